```python
import jax
import jax.numpy as jnp
from jax import lax
import numpy as np

D_MODEL = 2048
BATCH = 2
SEQ = 4096
DEPTH = 1
DEC_BATCH = 32
DEC_SEQ = 1
PAST_LEN = 8192
PAGE_SIZE = 128

N_META = 16
N_Q_HEADS = 8
N_KV_HEADS = 4
HEAD_DIM = 128
GROUP = N_Q_HEADS // N_KV_HEADS
ATTN_WIDTH = N_Q_HEADS * HEAD_DIM
KV_WIDTH = N_KV_HEADS * HEAD_DIM
N_IDX_HEADS = 8
IDX_DIM = 64
TOPK_MAX = 256
N_HG_HEADS = 8
HG_DK = 128
HG_DV = 128
HG_WIDTH = N_HG_HEADS * HG_DV
MIX_WIDTH = ATTN_WIDTH + HG_WIDTH
BLOCK = 128
D_FF = -(-8 * D_MODEL // (3 * 256)) * 256
IN_WIDTHS = (ATTN_WIDTH, KV_WIDTH, KV_WIDTH, N_IDX_HEADS * IDX_DIM, N_IDX_HEADS, IDX_DIM,
             N_HG_HEADS * HG_DK, N_HG_HEADS * HG_DK, HG_WIDTH, HG_WIDTH)
IN_WIDTH = sum(IN_WIDTHS)
ATTN_SCALE = HEAD_DIM ** -0.5
IDX_SCALE = IDX_DIM ** -0.5
IDX_W_SCALE = N_IDX_HEADS ** -0.5
LN_EPS = 1e-5
NORM_EPS = 1e-6
ALPHA = (2 * DEPTH) ** 0.25
BETA = (8 * DEPTH) ** -0.25

kernel_name = 'hymba_dsa_hgrn2_decode_step'


def layer_norm(x, g, b):
    xf = x.astype(jnp.float32)
    mu = jnp.mean(xf, -1, keepdims=True)
    var = jnp.mean(jnp.square(xf - mu), -1, keepdims=True)
    y = (xf - mu) * lax.rsqrt(var + LN_EPS) * g.astype(jnp.float32) + b.astype(jnp.float32)
    return y.astype(x.dtype)


def split_points():
    return [int(s) for s in np.cumsum(IN_WIDTHS)[:-1]]


def project(h, w_in, lb):
    B, T, _ = h.shape
    f32 = jnp.float32
    z = h @ w_in
    q, k, v, qi, wi, ki, hq, hf, hi, hg = jnp.split(z, split_points(), axis=-1)
    q = q.reshape(B, T, N_Q_HEADS, HEAD_DIM)
    k = k.reshape(B, T, N_KV_HEADS, HEAD_DIM)
    v = v.reshape(B, T, N_KV_HEADS, HEAD_DIM)
    qi = qi.reshape(B, T, N_IDX_HEADS, IDX_DIM)
    wi = wi.astype(f32) * IDX_W_SCALE
    fg = lb + (1.0 - lb) * jax.nn.sigmoid(hf.astype(f32))
    log_f = jnp.log(fg).reshape(B, T, N_HG_HEADS, HG_DK)
    kf = (1.0 - fg).reshape(B, T, N_HG_HEADS, HG_DK)
    hq = jax.nn.silu(hq.astype(f32)).reshape(B, T, N_HG_HEADS, HG_DK)
    hi = hi.astype(f32).reshape(B, T, N_HG_HEADS, HG_DV)
    return q, k, v, qi, wi, ki, hq, kf, hi, log_f, hg


def indexer_topk(qi, wi, q_pos, ki, n_keep):
    s = jnp.einsum('bqhd,bld->bqhl', qi, ki).astype(jnp.float32) * IDX_SCALE
    s = jnp.einsum('bqhl,bqh->bql', jax.nn.relu(s), wi)
    key_pos = jnp.arange(ki.shape[1])
    s = jnp.where(key_pos[None, None, :] <= q_pos[None, :, None], s, -jnp.inf)
    return lax.top_k(s, n_keep)[1]


def alibi_slopes():
    return jnp.exp2(-8.0 * jnp.arange(1, N_Q_HEADS + 1, dtype=jnp.float32) / N_Q_HEADS)


def attend(q, k_sel, v_sel, idx, q_pos):
    B, Q = q.shape[:2]
    qg = q.reshape(B, Q, N_KV_HEADS, GROUP, HEAD_DIM)
    logits = jnp.einsum('bqkgd,bqnkd->bqkgn', qg, k_sel).astype(jnp.float32) * ATTN_SCALE
    dist = (q_pos[None, :, None] - idx).astype(jnp.float32)[:, :, None, None, :]
    slopes = alibi_slopes().reshape(1, 1, N_KV_HEADS, GROUP, 1)
    logits = jnp.where(dist >= 0, logits - slopes * dist, -jnp.inf)
    p = jax.nn.softmax(logits, axis=-1).astype(v_sel.dtype)
    o = jnp.einsum('bqkgn,bqnkd->bqkgd', p, v_sel)
    return o.reshape(B, Q, ATTN_WIDTH)


def take_rows(rows, idx):
    return jax.vmap(lambda r, i: r[i])(rows, idx)


def gather_paged(pool, layer, page_table, new_rows, idx):
    is_past = idx < PAST_LEN
    ip = jnp.where(is_past, idx, 0)
    phys = page_table[jnp.arange(idx.shape[0])[:, None, None], ip // PAGE_SIZE]
    rows_past = pool[layer, phys, ip % PAGE_SIZE]
    rows_new = take_rows(new_rows, jnp.where(is_past, 0, idx - PAST_LEN))
    mask = is_past.reshape(is_past.shape + (1,) * (rows_past.ndim - is_past.ndim))
    return jnp.where(mask, rows_past, rows_new.astype(rows_past.dtype))


def hgrn2_chunk(S0, q, kf, i, log_f):
    C = q.shape[1]
    b = jnp.cumsum(log_f, axis=1)
    causal = jnp.tril(jnp.ones((C, C), dtype=bool))[None, :, :, None, None]
    decay = jnp.exp(jnp.where(causal, b[:, :, None] - b[:, None, :], -jnp.inf))
    a = jnp.einsum('btshd,bshd->bhts', q[:, :, None] * decay, kf)
    o = jnp.einsum('bhts,bshv->bthv', a, i) + jnp.einsum('bthd,bhdv->bthv', q * jnp.exp(b), S0)
    b_end = b[:, -1]
    S = S0 * jnp.exp(b_end)[..., None] + jnp.einsum('bshd,bshv->bhdv', kf * jnp.exp(b_end[:, None] - b), i)
    return S, o


def merge_heads(attn_o, hg_o, hg_gate, norm_g, w_out):
    B, T = attn_o.shape[:2]
    o = hg_o * lax.rsqrt(jnp.mean(jnp.square(hg_o), -1, keepdims=True) + NORM_EPS) * norm_g.astype(jnp.float32)
    o = o.reshape(B, T, HG_WIDTH) * jax.nn.silu(hg_gate.astype(jnp.float32))
    return jnp.concatenate([attn_o, o.astype(attn_o.dtype)], axis=-1) @ w_out


def swiglu(h, wg, wu, wd):
    return (jax.nn.silu(h @ wg) * (h @ wu)) @ wd


def to_blocks(a):
    r = a[:, N_META:]
    nb = r.shape[1] // BLOCK
    return jnp.moveaxis(r.reshape((r.shape[0], nb, BLOCK) + r.shape[2:]), 1, 0)


def from_blocks(a):
    a = jnp.moveaxis(a, 0, 1)
    return a.reshape((a.shape[0], a.shape[1] * a.shape[2]) + a.shape[3:])


def setup_inputs(seed: int = 0) -> dict:
    key = jax.random.key(seed)
    ks = jax.random.split(key, 24)
    n_pages = PAST_LEN // PAGE_SIZE
    n_pool = (DEC_BATCH * n_pages * 5 + 3) // 4
    f32 = jnp.float32

    def nrm(k, shape, scale):
        return jax.random.normal(k, shape, f32) * scale

    page_table = jax.random.permutation(ks[6], n_pool)[:DEC_BATCH * n_pages]
    page_table = page_table.reshape(DEC_BATCH, n_pages).astype(jnp.int32)
    return {
        'x_prompt': nrm(ks[0], (BATCH, SEQ, D_MODEL), 1.0),
        'x_sample': nrm(ks[1], (DEC_BATCH, DEC_SEQ, D_MODEL), 1.0),
        'cache_k': nrm(ks[2], (DEPTH, n_pool, PAGE_SIZE, N_KV_HEADS, HEAD_DIM), 1.0),
        'cache_v': nrm(ks[3], (DEPTH, n_pool, PAGE_SIZE, N_KV_HEADS, HEAD_DIM), 1.0),
        'cache_kidx': nrm(ks[4], (DEPTH, n_pool, PAGE_SIZE, IDX_DIM), 1.0),
        'state_hgrn': nrm(ks[5], (DEPTH, DEC_BATCH, N_HG_HEADS, HG_DK, HG_DV), 0.5),
        'page_table': page_table,
        'meta_tokens': nrm(ks[7], (N_META, D_MODEL), 1.0),
        'ln_in_g': 1.0 + nrm(ks[8], (D_MODEL,), 0.02),
        'ln_in_b': nrm(ks[9], (D_MODEL,), 0.02),
        'w_in': nrm(ks[10], (DEPTH, D_MODEL, IN_WIDTH), D_MODEL ** -0.5),
        'hg_lb_logits': nrm(ks[11], (DEPTH + 1, N_HG_HEADS * HG_DK), 0.5),
        'hg_norm_g': 1.0 + nrm(ks[12], (DEPTH, HG_DV), 0.02),
        'w_out': nrm(ks[13], (DEPTH, MIX_WIDTH, D_MODEL), MIX_WIDTH ** -0.5 * BETA),
        'ln1_g': 1.0 + nrm(ks[14], (DEPTH, D_MODEL), 0.02),
        'ln1_b': nrm(ks[15], (DEPTH, D_MODEL), 0.02),
        'w_gate': nrm(ks[16], (DEPTH, D_MODEL, D_FF), D_MODEL ** -0.5),
        'w_up': nrm(ks[17], (DEPTH, D_MODEL, D_FF), D_MODEL ** -0.5),
        'w_down': nrm(ks[18], (DEPTH, D_FF, D_MODEL), D_FF ** -0.5 * BETA),
        'ln2_g': 1.0 + nrm(ks[19], (DEPTH, D_MODEL), 0.02),
        'ln2_b': nrm(ks[20], (DEPTH, D_MODEL), 0.02),
    }


def reference(x_prompt, x_sample, cache_k, cache_v, cache_kidx, state_hgrn, page_table,
              meta_tokens, ln_in_g, ln_in_b, w_in, hg_lb_logits, hg_norm_g, w_out,
              ln1_g, ln1_b, w_gate, w_up, w_down, ln2_g, ln2_b):
    f32 = jnp.float32
    lb_all = jnp.cumsum(jax.nn.softmax(hg_lb_logits.astype(f32), axis=0), axis=0)

    B = x_prompt.shape[0]
    T = N_META + x_prompt.shape[1]
    meta = jnp.broadcast_to(meta_tokens[None], (B, N_META, D_MODEL)).astype(x_prompt.dtype)
    x = layer_norm(jnp.concatenate([meta, x_prompt], axis=1), ln_in_g, ln_in_b)
    n_keep_p = min(TOPK_MAX, T // 4)
    pos_p = jnp.arange(T)
    real_pos = pos_p[N_META:].reshape(-1, BLOCK)
    k_p, v_p, ki_p, s_p = [], [], [], []
    for l in range(DEPTH):
        q, k, v, qi, wi, ki, hq, kf, hi, log_f, hg = project(x, w_in[l], lb_all[l])

        def attn_block(args):
            qb, qib, wib, posb = args
            idx = indexer_topk(qib, wib, posb, ki, n_keep_p)
            return attend(qb, take_rows(k, idx), take_rows(v, idx), idx, posb)

        o_meta = attn_block((q[:, :N_META], qi[:, :N_META], wi[:, :N_META], pos_p[:N_META]))
        o_real = from_blocks(lax.map(attn_block, (to_blocks(q), to_blocks(qi), to_blocks(wi), real_pos)))
        attn_o = jnp.concatenate([o_meta, o_real], axis=1)

        S = jnp.zeros((B, N_HG_HEADS, HG_DK, HG_DV), f32)
        S, h_meta = hgrn2_chunk(S, hq[:, :N_META], kf[:, :N_META], hi[:, :N_META], log_f[:, :N_META])
        S, h_real = lax.scan(lambda c, a: hgrn2_chunk(c, *a), S,
                             (to_blocks(hq), to_blocks(kf), to_blocks(hi), to_blocks(log_f)))
        hg_o = jnp.concatenate([h_meta, from_blocks(h_real)], axis=1)

        mix = merge_heads(attn_o, hg_o, hg, hg_norm_g[l], w_out[l])
        x = layer_norm(ALPHA * x + mix, ln1_g[l], ln1_b[l])
        x = layer_norm(ALPHA * x + swiglu(x, w_gate[l], w_up[l], w_down[l]), ln2_g[l], ln2_b[l])
        k_p.append(k)
        v_p.append(v)
        ki_p.append(ki)
        s_p.append(S.astype(x_prompt.dtype))
    y_prompt = x[:, N_META:]

    xs = layer_norm(x_sample, ln_in_g, ln_in_b)
    DB, SQ = x_sample.shape[:2]
    n_keep_s = min(TOPK_MAX, (PAST_LEN + SQ) // 4)
    pos_s = PAST_LEN + jnp.arange(SQ)
    k_s, v_s, ki_s, s_s = [], [], [], []
    for l in range(DEPTH):
        q, k, v, qi, wi, ki, hq, kf, hi, log_f, hg = project(xs, w_in[l], lb_all[l])
        ki_past = cache_kidx[l, page_table].reshape(DB, PAST_LEN, IDX_DIM)
        ki_all = jnp.concatenate([ki_past, ki.astype(ki_past.dtype)], axis=1)
        idx = indexer_topk(qi, wi, pos_s, ki_all, n_keep_s)
        k_sel = gather_paged(cache_k, l, page_table, k, idx)
        v_sel = gather_paged(cache_v, l, page_table, v, idx)
        attn_o = attend(q, k_sel, v_sel, idx, pos_s)

        S, hg_o = hgrn2_chunk(state_hgrn[l].astype(f32), hq, kf, hi, log_f)

        mix = merge_heads(attn_o, hg_o, hg, hg_norm_g[l], w_out[l])
        xs = layer_norm(ALPHA * xs + mix, ln1_g[l], ln1_b[l])
        xs = layer_norm(ALPHA * xs + swiglu(xs, w_gate[l], w_up[l], w_down[l]), ln2_g[l], ln2_b[l])
        k_s.append(k)
        v_s.append(v)
        ki_s.append(ki)
        s_s.append(S.astype(state_hgrn.dtype))
    y_sample = xs

    return (y_prompt, y_sample,
            jnp.stack(k_p), jnp.stack(v_p), jnp.stack(ki_p), jnp.stack(s_p),
            jnp.stack(k_s), jnp.stack(v_s), jnp.stack(ki_s), jnp.stack(s_s))
```

```python
import functools

import numpy as np
import jax
import jax.numpy as jnp
from jax import lax
from jax.experimental import pallas as pl
from jax.experimental.pallas import tpu as pltpu

F32 = jnp.float32
BF16 = jnp.bfloat16
I32 = jnp.int32

D_MODEL = 2048
N_META = 16
BLK = 128
N_PAD = BLK - N_META
N_HEADS = 8
N_KV = 4
HEAD_DIM = 128
IDX_DIM = 64
TOPK = 256
PAGE = 128
LN_EPS = 1e-5
NORM_EPS = 1e-6
ALPHA = 2.0 ** 0.25
ATTN_SCALE = HEAD_DIM ** -0.5
IDX_SCALE = IDX_DIM ** -0.5
IDX_W_SCALE = N_HEADS ** -0.5
INT_MIN = -2 ** 31
INT_MAX = 2 ** 31 - 1
DEC_ROWS = 192

C_Q, C_HQ, C_HF, C_HI, C_HG = 0, 1024, 2048, 3072, 4096
C_K, C_V, C_QI, C_KK, C_WI = 5120, 5632, 6144, 6656, 6784
ZW = 6912

VMEM_LIMIT = 56 * 1024 * 1024


def _nt(a, b):
    return lax.dot_general(a, b, (((1,), (1,)), ((), ())), preferred_element_type=F32)


def _nn(a, b):
    return jnp.dot(a, b, preferred_element_type=F32)


def _layer_norm(x, g, b):
    mu = jnp.mean(x, axis=-1, keepdims=True)
    xc = x - mu
    var = jnp.mean(xc * xc, axis=-1, keepdims=True)
    return xc * lax.rsqrt(var + LN_EPS) * g + b


def _sort_key(s):
    s = jnp.where(s == 0.0, 0.0, s)
    bits = pltpu.bitcast(s, I32)
    return bits ^ ((bits >> 31) & INT_MAX)


def _inproj_kernel(x_ref, g_ref, b_ref, w_ref, z_ref, zb_ref, xs_ref):
    @pl.when(pl.program_id(1) == 0)
    def _():
        xs_ref[...] = _layer_norm(x_ref[...], g_ref[...], b_ref[...]).astype(BF16)

    acc = _nn(xs_ref[...], w_ref[...])
    z_ref[...] = acc
    zb_ref[...] = acc.astype(BF16)


def _inproj(x_all, g, b, w, tm, tn):
    r = x_all.shape[0]
    return pl.pallas_call(
        _inproj_kernel,
        grid=(r // tm, ZW // tn),
        in_specs=[
            pl.BlockSpec((tm, D_MODEL), lambda i, n: (i, 0)),
            pl.BlockSpec((1, D_MODEL), lambda i, n: (0, 0)),
            pl.BlockSpec((1, D_MODEL), lambda i, n: (0, 0)),
            pl.BlockSpec((D_MODEL, tn), lambda i, n: (0, n)),
        ],
        out_specs=[
            pl.BlockSpec((tm, tn), lambda i, n: (i, n)),
            pl.BlockSpec((tm, tn), lambda i, n: (i, n)),
        ],
        out_shape=[jax.ShapeDtypeStruct((r, ZW), F32), jax.ShapeDtypeStruct((r, ZW), BF16)],
        scratch_shapes=[pltpu.VMEM((tm, D_MODEL), BF16)],
        compiler_params=pltpu.CompilerParams(
            dimension_semantics=("arbitrary", "arbitrary"), vmem_limit_bytes=VMEM_LIMIT),
        name="inproj",
    )(x_all, g, b, w)


def _prompt_attn_kernel(q_ref, qi_ref, wi_ref, k_ref, v_ref, kk_ref, o_ref,
                        key_s, nd_s, lg_s, wb_s, qm_s, cut_s):
    j = pl.program_id(1)
    nt = j + 1
    row = lax.broadcasted_iota(I32, (BLK, BLK), 0)
    col = lax.broadcasted_iota(I32, (BLK, BLK), 1)
    qrow = j * BLK + row
    zeros = jnp.zeros((BLK, BLK), F32)

    wi = wi_ref[...] * (IDX_SCALE * IDX_W_SCALE)
    qi = qi_ref[...]
    for h in range(N_HEADS):
        wb_s[h] = jnp.broadcast_to(wi[:, h:h + 1], (BLK, BLK))
        q2 = qi[:, BLK * (h // 2):BLK * (h // 2 + 1)]
        keep = (col < IDX_DIM) if h % 2 == 0 else (col >= IDX_DIM)
        qm_s[h] = jnp.where(keep, q2, jnp.zeros_like(q2))

    def score_body(kt, carry):
        k0 = pl.multiple_of(kt * BLK, BLK)
        kk = kk_ref[pl.ds(k0, BLK), :]
        s = zeros
        for h in range(N_HEADS):
            s = s + jnp.maximum(_nt(qm_s[h], kk), 0.0) * wb_s[h]
        krow = k0 + col
        adm = (krow >= N_PAD) & (krow <= qrow)
        key_s[kt] = jnp.where(adm, _sort_key(s), INT_MIN)
        return carry

    lax.fori_loop(0, nt, score_body, 0)

    def count(pred):
        acc = lax.fori_loop(0, nt, lambda kt, a: a + jnp.where(pred(kt, key_s[kt]), 1.0, 0.0), zeros)
        return jnp.sum(acc, axis=-1, keepdims=True)

    def radix_body(i, p):
        cand = p + lax.shift_left(jnp.int32(1), 31 - i)
        candb = jnp.broadcast_to(cand, (BLK, BLK))
        return jnp.where(count(lambda kt, key: key >= candb) >= TOPK, cand, p)

    p = lax.fori_loop(0, 32, radix_body, jnp.full((BLK, 1), INT_MIN, I32))
    pb = jnp.broadcast_to(p, (BLK, BLK))
    n_gt = count(lambda kt, key: key > pb)
    n_ge = count(lambda kt, key: key >= pb)
    need = TOPK - n_gt
    overflow = (n_ge > TOPK) & (p > INT_MIN)

    cut_s[...] = jnp.full((BLK, BLK), INT_MAX, I32)

    @pl.when(jnp.max(jnp.where(overflow, 1.0, 0.0)) > 0.0)
    def _():
        def tie_body(i, x):
            cand = x + lax.shift_left(jnp.int32(1), 12 - i)
            candb = jnp.broadcast_to(cand, (BLK, BLK))
            g = count(lambda kt, key: (key == pb) & (kt * BLK + col < candb))
            return jnp.where(g < need, cand, x)

        x = lax.fori_loop(0, 13, tie_body, jnp.zeros((BLK, 1), I32))
        cut_s[...] = jnp.broadcast_to(jnp.where(overflow, x, INT_MAX), (BLK, BLK))

    cutb = cut_s[...]

    def nd_body(kt, carry):
        key = key_s[kt]
        krow = kt * BLK + col
        sel = ((key > pb) | ((key == pb) & (krow <= cutb))) & (key > INT_MIN)
        nd_s[kt] = jnp.where(sel, -(qrow - krow).astype(F32), -jnp.inf)
        return carry

    lax.fori_loop(0, nt, nd_body, 0)

    for h in range(N_HEADS):
        kvh = h // (N_HEADS // N_KV)
        slope = 2.0 ** -(h + 1)
        qh = q_ref[:, HEAD_DIM * h:HEAD_DIM * (h + 1)]

        def logit_body(kt, m):
            k0 = pl.multiple_of(kt * BLK, BLK)
            kt_tile = k_ref[pl.ds(k0, BLK), HEAD_DIM * kvh:HEAD_DIM * (kvh + 1)]
            lg = _nt(qh, kt_tile) * ATTN_SCALE + slope * nd_s[kt]
            lg_s[kt] = lg
            return jnp.maximum(m, lg)

        mrun = lax.fori_loop(0, nt, logit_body, jnp.full((BLK, BLK), -jnp.inf, F32))
        m = jnp.max(mrun, axis=-1, keepdims=True)
        mb = jnp.broadcast_to(jnp.where(m == -jnp.inf, 0.0, m), (BLK, BLK))

        def pv_body(kt, carry):
            l, acc = carry
            k0 = pl.multiple_of(kt * BLK, BLK)
            pr = jnp.exp(lg_s[kt] - mb)
            v_tile = v_ref[pl.ds(k0, BLK), HEAD_DIM * kvh:HEAD_DIM * (kvh + 1)]
            return l + pr, acc + _nn(pr.astype(BF16), v_tile)

        l, acc = lax.fori_loop(0, nt, pv_body, (zeros, zeros))
        lsum = jnp.sum(l, axis=-1, keepdims=True)
        lsum = jnp.where(lsum == 0.0, 1.0, lsum)
        o_ref[:, HEAD_DIM * h:HEAD_DIM * (h + 1)] = (acc / lsum).astype(BF16)


def _prompt_attn(z, zb, nbatch, tp):
    nb = tp // BLK
    return pl.pallas_call(
        _prompt_attn_kernel,
        grid=(nbatch, nb),
        in_specs=[
            pl.BlockSpec((BLK, 1024), lambda b, j: (b * nb + j, C_Q // 1024)),
            pl.BlockSpec((BLK, 512), lambda b, j: (b * nb + j, C_QI // 512)),
            pl.BlockSpec((BLK, 128), lambda b, j: (b * nb + j, C_WI // 128)),
            pl.BlockSpec((tp, 512), lambda b, j: (b, C_K // 512)),
            pl.BlockSpec((tp, 512), lambda b, j: (b, C_V // 512)),
            pl.BlockSpec((tp, 128), lambda b, j: (b, C_KK // 128)),
        ],
        out_specs=pl.BlockSpec((BLK, 1024), lambda b, j: (b * nb + j, 0)),
        out_shape=jax.ShapeDtypeStruct((nbatch * tp, 1024), BF16),
        scratch_shapes=[
            pltpu.VMEM((nb, BLK, BLK), I32),
            pltpu.VMEM((nb, BLK, BLK), F32),
            pltpu.VMEM((nb, BLK, BLK), F32),
            pltpu.VMEM((N_HEADS, BLK, BLK), F32),
            pltpu.VMEM((N_HEADS, BLK, BLK), BF16),
            pltpu.VMEM((BLK, BLK), I32),
        ],
        compiler_params=pltpu.CompilerParams(
            dimension_semantics=("arbitrary", "arbitrary"), vmem_limit_bytes=VMEM_LIMIT),
        name="prompt_attn",
    )(zb, zb, z, zb, zb, zb)


def _scan_matrix():
    t = np.tril(np.ones((BLK, BLK), np.float32))
    mats = [t]
    for lvl in range(1, 8):
        bs = BLK >> (lvl - 1)
        r = np.arange(BLK)
        anchor = (r // bs) * bs + bs // 2 - 1
        mats.append(t[anchor])
    return np.concatenate(mats, axis=0)


def _prompt_hgrn_kernel(hq_ref, hf_ref, hi_ref, hg_ref, lbl_ref, ng_ref, scan_ref,
                        o_ref, s_ref, st_s):
    c = pl.program_id(1)
    nc = pl.num_programs(1)

    @pl.when(c == 0)
    def _():
        st_s[...] = jnp.zeros_like(st_s)

    row = lax.broadcasted_iota(I32, (BLK, BLK), 0)
    col = lax.broadcasted_iota(I32, (BLK, BLK), 1)
    valid = (c > 0) | (row >= N_PAD)
    lower = col < row
    x = row ^ col
    ll0, ll1 = lbl_ref[0:1, :], lbl_ref[1:2, :]
    mx = jnp.maximum(ll0, ll1)
    e0, e1 = jnp.exp(ll0 - mx), jnp.exp(ll1 - mx)
    lb_all = e0 / (e0 + e1)
    scan = scan_ref[...]
    ng = ng_ref[...]

    for h in range(N_HEADS):
        sl = slice(HEAD_DIM * h, HEAD_DIM * (h + 1))
        lb = lb_all[:, sl]
        f = lb + (1.0 - lb) * jax.nn.sigmoid(hf_ref[:, sl])
        lf = jnp.where(valid, jnp.log(f), 0.0)
        kf = jnp.where(valid, 1.0 - f, 0.0)
        hq = hq_ref[:, sl]
        qh = hq * jax.nn.sigmoid(hq)
        ih = hi_ref[:, sl].astype(BF16)

        p0 = lf.astype(BF16)
        r1 = lf - p0.astype(F32)
        p1 = r1.astype(BF16)
        p2 = (r1 - p1.astype(F32)).astype(BF16)
        ba = _nn(scan, jnp.concatenate([p0, p1, p2], axis=1))
        ba = ba[:, 0:BLK] + ba[:, BLK:2 * BLK] + ba[:, 2 * BLK:3 * BLK]
        b = ba[0:BLK]

        a = jnp.where(row == col, _nt(qh.astype(BF16), kf.astype(BF16)), 0.0)
        for lvl in range(1, 8):
            dec = jnp.exp(-jnp.abs(b - ba[BLK * lvl:BLK * (lvl + 1)]))
            qk = _nt((qh * dec).astype(BF16), (kf * dec).astype(BF16))
            a = a + jnp.where(lower & ((x >> (7 - lvl)) == 1), qk, 0.0)

        st = st_s[h]
        o = _nn(a.astype(BF16), ih) + _nt((qh * jnp.exp(b)).astype(BF16), st.astype(BF16))
        bend = b[BLK - 1:BLK, :]
        kend = (kf * jnp.exp(bend - b)).astype(BF16)
        st_new = st * jnp.exp(bend) + _nn(hi_ref[:, sl].T.astype(BF16), kend)
        st_s[h] = st_new

        on = o * lax.rsqrt(jnp.mean(o * o, axis=-1, keepdims=True) + NORM_EPS) * ng
        hg = hg_ref[:, sl]
        o_ref[:, sl] = (on * (hg * jax.nn.sigmoid(hg))).astype(BF16)

        @pl.when(c == nc - 1)
        def _():
            s_ref[0, h] = st_new.T


def _prompt_hgrn(z, lb_logits, norm_g, nbatch, tp):
    nb = tp // BLK
    scan = jnp.asarray(_scan_matrix(), BF16)
    row_spec = lambda cb: pl.BlockSpec((BLK, 1024), lambda b, c: (b * nb + c, cb))
    return pl.pallas_call(
        _prompt_hgrn_kernel,
        grid=(nbatch, nb),
        in_specs=[
            row_spec(C_HQ // 1024), row_spec(C_HF // 1024), row_spec(C_HI // 1024), row_spec(C_HG // 1024),
            pl.BlockSpec((2, 1024), lambda b, c: (0, 0)),
            pl.BlockSpec((1, HEAD_DIM), lambda b, c: (0, 0)),
            pl.BlockSpec((8 * BLK, BLK), lambda b, c: (0, 0)),
        ],
        out_specs=[
            pl.BlockSpec((BLK, 1024), lambda b, c: (b * nb + c, 0)),
            pl.BlockSpec((1, N_HEADS, HEAD_DIM, HEAD_DIM), lambda b, c: (b, 0, 0, 0)),
        ],
        out_shape=[
            jax.ShapeDtypeStruct((nbatch * tp, 1024), BF16),
            jax.ShapeDtypeStruct((nbatch, N_HEADS, HEAD_DIM, HEAD_DIM), F32),
        ],
        scratch_shapes=[pltpu.VMEM((N_HEADS, HEAD_DIM, HEAD_DIM), F32)],
        compiler_params=pltpu.CompilerParams(
            dimension_semantics=("arbitrary", "arbitrary"), vmem_limit_bytes=VMEM_LIMIT),
        name="prompt_hgrn",
    )(z, z, z, z, lb_logits, norm_g, scan)


PG = 8


def _dec_idx_proj_kernel(x_ref, g_ref, b_ref, w_ref, o_ref):
    xn = _layer_norm(x_ref[...], g_ref[...], b_ref[...])
    o_ref[...] = jnp.dot(xn, w_ref[...], preferred_element_type=F32, precision=lax.Precision.HIGHEST)


def _dec_idx_proj(x_s, g, b, w_idx):
    return pl.pallas_call(
        _dec_idx_proj_kernel,
        out_shape=jax.ShapeDtypeStruct((x_s.shape[0], w_idx.shape[1]), F32),
        compiler_params=pltpu.CompilerParams(vmem_limit_bytes=VMEM_LIMIT),
        name="dec_idx_proj",
    )(x_s, g, b, w_idx)


def _dec_score_kernel(pt_ref, qi_ref, w_ref, *refs):
    kp_refs, out_ref = refs[:PG], refs[PG]
    qi = qi_ref[0]
    w = w_ref[0] * (IDX_SCALE * IDX_W_SCALE)
    for i in range(PG):
        qk = lax.dot_general(qi, kp_refs[i][0, 0], (((1,), (1,)), ((), ())),
                             preferred_element_type=F32, precision=lax.Precision.HIGHEST)
        out_ref[0, i] = jnp.sum(jnp.maximum(qk, 0.0) * w, axis=0, keepdims=True)


def _dec_scores(page_table, qi_s, wcol, cache_kidx):
    nbatch, npages = page_table.shape
    page_spec = lambda i: pl.BlockSpec(
        (1, 1, PAGE, IDX_DIM), lambda b, p, pt: (0, pt[b, p * PG + i], 0, 0))
    return pl.pallas_call(
        _dec_score_kernel,
        grid_spec=pltpu.PrefetchScalarGridSpec(
            num_scalar_prefetch=1,
            grid=(nbatch, npages // PG),
            in_specs=[
                pl.BlockSpec((1, N_HEADS, IDX_DIM), lambda b, p, pt: (b, 0, 0)),
                pl.BlockSpec((1, N_HEADS, 1), lambda b, p, pt: (b, 0, 0)),
            ] + [page_spec(i) for i in range(PG)],
            out_specs=pl.BlockSpec((1, PG, 1, PAGE), lambda b, p, pt: (b, p, 0, 0)),
        ),
        out_shape=jax.ShapeDtypeStruct((nbatch, npages, 1, PAGE), F32),
        compiler_params=pltpu.CompilerParams(dimension_semantics=("arbitrary", "arbitrary")),
        name="dec_scores",
    )(page_table, qi_s, wcol, *([cache_kidx] * PG))


def _dec_select_kernel(sc_ref, qi_ref, kin_ref, w_ref, nd_ref, ndn_ref, key_s):
    nq, npast = sc_ref.shape
    col = lax.broadcasted_iota(I32, (nq, npast), 1)
    key_s[...] = _sort_key(sc_ref[...])

    kin = kin_ref[...]
    w = w_ref[...] * (IDX_SCALE * IDX_W_SCALE)
    snew = jnp.zeros((nq, 1), F32)
    for h in range(N_HEADS):
        sh = jnp.sum(qi_ref[h] * kin, axis=-1, keepdims=True)
        snew = snew + jnp.maximum(sh, 0.0) * w[:, h:h + 1]
    keyn = _sort_key(snew)

    def count(pred_past, pred_new):
        c = jnp.sum(jnp.where(pred_past(key_s[...]), 1.0, 0.0), axis=-1, keepdims=True)
        return c + jnp.where(pred_new(keyn), 1.0, 0.0)

    def radix_body(i, p):
        cand = p + lax.shift_left(jnp.int32(1), 31 - i)
        return jnp.where(count(lambda k: k >= cand, lambda k: k >= cand) >= TOPK, cand, p)

    p = lax.fori_loop(0, 32, radix_body, jnp.full((nq, 1), INT_MIN, I32))
    n_gt = count(lambda k: k > p, lambda k: k > p)
    n_ge = count(lambda k: k >= p, lambda k: k >= p)
    need = TOPK - n_gt
    overflow = n_ge > TOPK

    def tie_body(i, x):
        cand = x + lax.shift_left(jnp.int32(1), 13 - i)
        g = count(lambda k: (k == p) & (col < cand), lambda k: (k == p) & (npast < cand))
        return jnp.where(g < need, cand, x)

    x = lax.fori_loop(0, 14, tie_body, jnp.zeros((nq, 1), I32))
    cut = jnp.where(overflow, x, INT_MAX)

    key = key_s[...]
    sel = (key > p) | ((key == p) & (col <= cut))
    nd_ref[...] = jnp.where(sel, -(npast - col).astype(F32), -jnp.inf)
    seln = (keyn > p) | ((keyn == p) & (npast <= cut))
    lane = lax.broadcasted_iota(I32, ndn_ref.shape, 1)
    ndn_ref[...] = jnp.where((lane == 0) & seln, 0.0, -jnp.inf)


def _dec_select(sc, qi_hm, ki_new, wrow):
    nq, npast = sc.shape
    return pl.pallas_call(
        _dec_select_kernel,
        out_shape=[jax.ShapeDtypeStruct((nq, npast), F32), jax.ShapeDtypeStruct((nq, PAGE), F32)],
        scratch_shapes=[pltpu.VMEM((nq, npast), I32)],
        compiler_params=pltpu.CompilerParams(vmem_limit_bytes=VMEM_LIMIT),
        name="dec_select",
    )(sc, qi_hm, ki_new, wrow)


def _dec_attn_kernel(pt_ref, q_ref, nd_ref, ndn_ref, kn_ref, vn_ref, *refs):
    k_refs, v_refs = refs[:PG], refs[PG:2 * PG]
    o_ref, m_s, l_s, acc_s = refs[2 * PG:]
    p = pl.program_id(1)
    hrow = lax.broadcasted_iota(I32, (N_HEADS, PAGE), 0)
    slope = jnp.exp2(-(hrow + 1).astype(F32))
    group = N_HEADS // N_KV
    q = q_ref[0]

    def update(lgs, pv):
        m_old = m_s[...]
        mx = lgs[0]
        for t in lgs[1:]:
            mx = jnp.maximum(mx, t)
        m_new = jnp.maximum(m_old, jnp.broadcast_to(jnp.max(mx, axis=-1, keepdims=True), m_old.shape))
        m_safe = jnp.where(m_new == -jnp.inf, 0.0, m_new)
        alpha = jnp.exp(m_old - m_safe)
        ps = [jnp.exp(t - m_safe) for t in lgs]
        ls = ps[0]
        for t in ps[1:]:
            ls = ls + t
        l_s[...] = alpha * l_s[...] + jnp.broadcast_to(jnp.sum(ls, axis=-1, keepdims=True), m_old.shape)
        acc_s[...] = alpha * acc_s[...] + pv(ps)
        m_s[...] = m_new

    @pl.when(p == 0)
    def _():
        m_s[...] = jnp.full_like(m_s, -jnp.inf)
        l_s[...] = jnp.zeros_like(l_s)
        acc_s[...] = jnp.zeros_like(acc_s)
        ln = jnp.sum(q.astype(F32) * kn_ref[0].astype(F32), axis=-1, keepdims=True) * ATTN_SCALE
        lg = jnp.broadcast_to(ln, (N_HEADS, PAGE)) + ndn_ref[0]
        vn = vn_ref[0].astype(F32)
        update([lg], lambda ps: jnp.sum(ps[0], axis=-1, keepdims=True) * vn)

    lgs = []
    for i in range(PG):
        lg = jnp.zeros((N_HEADS, PAGE), F32)
        for kvh in range(N_KV):
            kt = k_refs[i][0, 0, :, kvh, :].astype(BF16)
            lg = jnp.where(hrow // group == kvh, _nt(q, kt), lg)
        lgs.append(lg * ATTN_SCALE + slope * nd_ref[0, i])

    def pv(ps):
        out = jnp.zeros((N_HEADS, HEAD_DIM), F32)
        for i in range(PG):
            pb = ps[i].astype(BF16)
            for kvh in range(N_KV):
                vt = v_refs[i][0, 0, :, kvh, :].astype(BF16)
                out = out + jnp.where(hrow // group == kvh, _nn(pb, vt), 0.0)
        return out

    update(lgs, pv)

    @pl.when(p == pl.num_programs(1) - 1)
    def _():
        l = l_s[...]
        o_ref[0] = (acc_s[...] / jnp.where(l == 0.0, 1.0, l)).astype(BF16)


def _dec_attn(page_table, q8, nd, ndn, kn8, vn8, cache_k, cache_v):
    nbatch, npages = page_table.shape
    page_spec = lambda i: pl.BlockSpec(
        (1, 1, PAGE, N_KV, HEAD_DIM), lambda b, p, pt: (0, pt[b, p * PG + i], 0, 0, 0))
    head_spec = pl.BlockSpec((1, N_HEADS, HEAD_DIM), lambda b, p, pt: (b, 0, 0))
    return pl.pallas_call(
        _dec_attn_kernel,
        grid_spec=pltpu.PrefetchScalarGridSpec(
            num_scalar_prefetch=1,
            grid=(nbatch, npages // PG),
            in_specs=[
                head_spec,
                pl.BlockSpec((1, PG, 1, PAGE), lambda b, p, pt: (b, p, 0, 0)),
                pl.BlockSpec((1, 1, PAGE), lambda b, p, pt: (b, 0, 0)),
                head_spec, head_spec,
            ] + [page_spec(i) for i in range(PG)] + [page_spec(i) for i in range(PG)],
            out_specs=head_spec,
            scratch_shapes=[pltpu.VMEM((N_HEADS, PAGE), F32)] * 3,
        ),
        out_shape=jax.ShapeDtypeStruct((nbatch, N_HEADS, HEAD_DIM), BF16),
        compiler_params=pltpu.CompilerParams(
            dimension_semantics=("arbitrary", "arbitrary"), vmem_limit_bytes=VMEM_LIMIT),
        name="dec_attn",
    )(page_table, q8, nd, ndn, kn8, vn8, *([cache_k] * PG), *([cache_v] * PG))


def _dec_hgrn_kernel(s_ref, hq_ref, hf_ref, hi_ref, hg_ref, lbl_ref, ng_ref, so_ref, o_ref):
    ll0, ll1 = lbl_ref[0], lbl_ref[1]
    mx = jnp.maximum(ll0, ll1)
    e0, e1 = jnp.exp(ll0 - mx), jnp.exp(ll1 - mx)
    lb = e0 / (e0 + e1)
    f = lb + (1.0 - lb) * jax.nn.sigmoid(hf_ref[0])
    hq = hq_ref[0]
    q = hq * jax.nn.sigmoid(hq)
    ng = ng_ref[...]
    for h in range(N_HEADS):
        s_new = s_ref[0, 0, h] * f[h] + (1.0 - f[h]) * hi_ref[0, h]
        so_ref[0, h] = s_new
        o = jnp.sum(s_new * q[h], axis=0, keepdims=True)
        on = o * lax.rsqrt(jnp.mean(o * o, axis=-1, keepdims=True) + NORM_EPS) * ng
        hg = hg_ref[0, h]
        o_ref[0, h] = (on * (hg * jax.nn.sigmoid(hg))).astype(BF16)


def _dec_hgrn(state, hq_c, hf_c, hi_r, hg_r, lbl_c, norm_g):
    nbatch = state.shape[1]
    col_spec = pl.BlockSpec((1, N_HEADS, HEAD_DIM, 1), lambda b: (b, 0, 0, 0))
    row_spec = pl.BlockSpec((1, N_HEADS, 1, HEAD_DIM), lambda b: (b, 0, 0, 0))
    return pl.pallas_call(
        _dec_hgrn_kernel,
        grid=(nbatch,),
        in_specs=[
            pl.BlockSpec((1, 1, N_HEADS, HEAD_DIM, HEAD_DIM), lambda b: (0, b, 0, 0, 0)),
            col_spec, col_spec, row_spec, row_spec,
            pl.BlockSpec((2, N_HEADS, HEAD_DIM, 1), lambda b: (0, 0, 0, 0)),
            pl.BlockSpec((1, HEAD_DIM), lambda b: (0, 0)),
        ],
        out_specs=[
            pl.BlockSpec((1, N_HEADS, HEAD_DIM, HEAD_DIM), lambda b: (b, 0, 0, 0)),
            row_spec,
        ],
        out_shape=[
            jax.ShapeDtypeStruct((nbatch, N_HEADS, HEAD_DIM, HEAD_DIM), F32),
            jax.ShapeDtypeStruct((nbatch, N_HEADS, 1, HEAD_DIM), BF16),
        ],
        compiler_params=pltpu.CompilerParams(dimension_semantics=("arbitrary",)),
        name="dec_hgrn",
    )(state, hq_c, hf_c, hi_r, hg_r, lbl_c, norm_g)


def _outproj_kernel(x_ref, gi_ref, bi_ref, a_ref, h_ref, wa_ref, wh_ref, g_ref, b_ref, o_ref):
    xn = _layer_norm(x_ref[...], gi_ref[...], bi_ref[...])
    mix = _nn(a_ref[...], wa_ref[...]) + _nn(h_ref[...], wh_ref[...])
    o_ref[...] = _layer_norm(ALPHA * xn + mix, g_ref[...], b_ref[...])


def _outproj(x_all, gi, bi, attn, hgm, wa, wh, g, b, tm):
    r = x_all.shape[0]
    vec = pl.BlockSpec((1, D_MODEL), lambda i: (0, 0))
    return pl.pallas_call(
        _outproj_kernel,
        grid=(r // tm,),
        in_specs=[
            pl.BlockSpec((tm, D_MODEL), lambda i: (i, 0)), vec, vec,
            pl.BlockSpec((tm, 1024), lambda i: (i, 0)),
            pl.BlockSpec((tm, 1024), lambda i: (i, 0)),
            pl.BlockSpec((1024, D_MODEL), lambda i: (0, 0)),
            pl.BlockSpec((1024, D_MODEL), lambda i: (0, 0)),
            vec, vec,
        ],
        out_specs=pl.BlockSpec((tm, D_MODEL), lambda i: (i, 0)),
        out_shape=jax.ShapeDtypeStruct((r, D_MODEL), F32),
        compiler_params=pltpu.CompilerParams(
            dimension_semantics=("arbitrary",), vmem_limit_bytes=VMEM_LIMIT),
        name="outproj",
    )(x_all, gi, bi, attn, hgm, wa, wh, g, b)


def _ffn_kernel(x_ref, wg_ref, wu_ref, wd_ref, g_ref, b_ref, y_ref, xb_ref):
    f = pl.program_id(1)

    @pl.when(f == 0)
    def _():
        xb_ref[...] = x_ref[...].astype(BF16)
        y_ref[...] = jnp.zeros_like(y_ref)

    xb = xb_ref[...]
    gate = _nn(xb, wg_ref[...])
    up = _nn(xb, wu_ref[...])
    hid = (gate * jax.nn.sigmoid(gate) * up).astype(BF16)
    y_ref[...] += _nn(hid, wd_ref[...])

    @pl.when(f == pl.num_programs(1) - 1)
    def _():
        y_ref[...] = _layer_norm(ALPHA * x_ref[...] + y_ref[...], g_ref[...], b_ref[...])


def _ffn(x1, wg, wu, wd, g, b, tm, tf):
    r = x1.shape[0]
    dff = wg.shape[1]
    vec = pl.BlockSpec((1, D_MODEL), lambda i, f: (0, 0))
    return pl.pallas_call(
        _ffn_kernel,
        grid=(r // tm, dff // tf),
        in_specs=[
            pl.BlockSpec((tm, D_MODEL), lambda i, f: (i, 0)),
            pl.BlockSpec((D_MODEL, tf), lambda i, f: (0, f)),
            pl.BlockSpec((D_MODEL, tf), lambda i, f: (0, f)),
            pl.BlockSpec((tf, D_MODEL), lambda i, f: (f, 0)),
            vec, vec,
        ],
        out_specs=pl.BlockSpec((tm, D_MODEL), lambda i, f: (i, 0)),
        out_shape=jax.ShapeDtypeStruct((r, D_MODEL), F32),
        scratch_shapes=[pltpu.VMEM((tm, D_MODEL), BF16)],
        compiler_params=pltpu.CompilerParams(
            dimension_semantics=("arbitrary", "arbitrary"), vmem_limit_bytes=VMEM_LIMIT),
        name="ffn",
    )(x1, wg, wu, wd, g, b)


def _reorder_w_in(w):
    q, k, v, qi, wi, ki, hq, hf, hi, hg = jnp.split(
        w, [1024, 1536, 2048, 2560, 2568, 2632, 3656, 4680, 5704], axis=1)
    pad = jnp.zeros((w.shape[0], 128 - wi.shape[1]), w.dtype)
    return jnp.concatenate([q, hq, hf, hi, hg, k, v, qi, ki, ki, wi, pad], axis=1).astype(BF16)


def kernel(x_prompt, x_sample, cache_k, cache_v, cache_kidx, state_hgrn, page_table, meta_tokens,
           ln_in_g, ln_in_b, w_in, hg_lb_logits, hg_norm_g, w_out, ln1_g, ln1_b, w_gate, w_up,
           w_down, ln2_g, ln2_b):
    nbatch, seq, _ = x_prompt.shape
    ndec = x_sample.shape[0]
    assert w_in.shape[0] == 1 and x_sample.shape[1] == 1 and seq % BLK == 0
    tp = BLK + seq
    rp = nbatch * tp
    ntail = DEC_ROWS - ndec
    r = rp + DEC_ROWS
    assert ntail >= 0 and r % (20 * 16) == 0

    head = jnp.concatenate([jnp.zeros((N_PAD, D_MODEL), F32), meta_tokens.astype(F32)], axis=0)
    head = jnp.broadcast_to(head[None], (nbatch, BLK, D_MODEL))
    x_all = jnp.concatenate([head, x_prompt], axis=1).reshape(rp, D_MODEL)
    x_all = jnp.concatenate(
        [x_all, x_sample.reshape(ndec, D_MODEL), jnp.zeros((ntail, D_MODEL), F32)], axis=0)

    row = lambda a: a.reshape(1, -1).astype(F32)
    tm = r // 10
    z, zb = _inproj(x_all, row(ln_in_g), row(ln_in_b), _reorder_w_in(w_in[0]), tm, 1152)

    lbl = hg_lb_logits.astype(F32)
    ng = row(hg_norm_g[0])
    attn_p = _prompt_attn(z, zb, nbatch, tp)
    hgm_p, s_p = _prompt_hgrn(z, lbl, ng, nbatch, tp)

    zs, zbs = z[rp:rp + ndec], zb[rp:rp + ndec]
    npast = page_table.shape[1] * PAGE
    wi0 = w_in[0]
    w_idx = jnp.concatenate([wi0[:, 2048:2560], wi0[:, 2568:2632], wi0[:, 2560:2568],
                             jnp.zeros((D_MODEL, 56), F32)], axis=1)
    z_idx = _dec_idx_proj(x_sample.reshape(ndec, D_MODEL), row(ln_in_g), row(ln_in_b), w_idx)
    qi_s = z_idx[:, 0:512].reshape(ndec, N_HEADS, IDX_DIM)
    wrow = z_idx[:, 576:576 + N_HEADS]
    sc = _dec_scores(page_table, qi_s, wrow[:, :, None], cache_kidx).reshape(ndec, npast)
    nd, ndn = _dec_select(sc, jnp.swapaxes(qi_s, 0, 1), z_idx[:, 512:576], wrow)
    rep = lambda a: jnp.repeat(a.reshape(ndec, N_KV, HEAD_DIM), N_HEADS // N_KV, axis=1)
    attn_s = _dec_attn(page_table, zbs[:, C_Q:C_Q + 1024].reshape(ndec, N_HEADS, HEAD_DIM),
                       nd.reshape(ndec, npast // PAGE, 1, PAGE), ndn.reshape(ndec, 1, PAGE),
                       rep(zbs[:, C_K:C_K + 512]), rep(zbs[:, C_V:C_V + 512]), cache_k, cache_v)
    colv = lambda c0: zs[:, c0:c0 + 1024].reshape(ndec, N_HEADS, HEAD_DIM, 1)
    rowv = lambda c0: zs[:, c0:c0 + 1024].reshape(ndec, N_HEADS, 1, HEAD_DIM)
    s_s, hgm_s = _dec_hgrn(state_hgrn, colv(C_HQ), colv(C_HF), rowv(C_HI), rowv(C_HG),
                           lbl.reshape(2, N_HEADS, HEAD_DIM, 1), ng)

    tail = jnp.zeros((ntail, 1024), BF16)
    attn = jnp.concatenate([attn_p, attn_s.reshape(ndec, 1024), tail], axis=0)
    hgm = jnp.concatenate([hgm_p, hgm_s.reshape(ndec, 1024), tail], axis=0)
    wo = w_out[0].astype(BF16)
    x1 = _outproj(x_all, row(ln_in_g), row(ln_in_b), attn, hgm, wo[:1024], wo[1024:],
                  row(ln1_g[0]), row(ln1_b[0]), r // 20)
    y = _ffn(x1, w_gate[0].astype(BF16), w_up[0].astype(BF16), w_down[0].astype(BF16),
             row(ln2_g[0]), row(ln2_b[0]), tm, 256)

    zp = z[:rp].reshape(nbatch, tp, ZW)[:, N_PAD:]
    y_prompt = y[:rp].reshape(nbatch, tp, D_MODEL)[:, BLK:]
    y_sample = y[rp:rp + ndec].reshape(ndec, 1, D_MODEL)
    k_p = zp[:, :, C_K:C_K + 512].reshape(1, nbatch, tp - N_PAD, N_KV, HEAD_DIM)
    v_p = zp[:, :, C_V:C_V + 512].reshape(1, nbatch, tp - N_PAD, N_KV, HEAD_DIM)
    ki_p = zp[:, :, C_KK:C_KK + IDX_DIM][None]
    k_s = zs[:, C_K:C_K + 512].reshape(1, ndec, 1, N_KV, HEAD_DIM)
    v_s = zs[:, C_V:C_V + 512].reshape(1, ndec, 1, N_KV, HEAD_DIM)
    ki_s = zs[:, C_KK:C_KK + IDX_DIM].reshape(1, ndec, 1, IDX_DIM)
    return (y_prompt, y_sample, k_p, v_p, ki_p, s_p[None], k_s, v_s, ki_s, s_s[None])
```

```python
import functools

import numpy as np
import jax
import jax.numpy as jnp
from jax import lax
from jax.experimental import pallas as pl
from jax.experimental.pallas import tpu as pltpu

F32 = jnp.float32
BF16 = jnp.bfloat16
I32 = jnp.int32

D_MODEL = 2048
N_META = 16
BLK = 128
N_PAD = BLK - N_META
N_HEADS = 8
N_KV = 4
HEAD_DIM = 128
IDX_DIM = 64
TOPK = 256
PAGE = 128
LN_EPS = 1e-5
NORM_EPS = 1e-6
ALPHA = 2.0 ** 0.25
ATTN_SCALE = HEAD_DIM ** -0.5
IDX_SCALE = IDX_DIM ** -0.5
IDX_W_SCALE = N_HEADS ** -0.5
INT_MIN = -2 ** 31
INT_MAX = 2 ** 31 - 1
DEC_ROWS = 192

C_Q, C_HQ, C_HF, C_HI, C_HG = 0, 1024, 2048, 3072, 4096
C_K, C_V, C_QI, C_KK, C_WI = 5120, 5632, 6144, 6656, 6784
ZW = 6912

VMEM_LIMIT = 56 * 1024 * 1024


def _nt(a, b):
    return lax.dot_general(a, b, (((1,), (1,)), ((), ())), preferred_element_type=F32)


def _nn(a, b):
    return jnp.dot(a, b, preferred_element_type=F32)


def _layer_norm(x, g, b):
    mu = jnp.mean(x, axis=-1, keepdims=True)
    xc = x - mu
    var = jnp.mean(xc * xc, axis=-1, keepdims=True)
    return xc * lax.rsqrt(var + LN_EPS) * g + b


def _sort_key(s):
    s = jnp.where(s == 0.0, 0.0, s)
    bits = pltpu.bitcast(s, I32)
    return bits ^ ((bits >> 31) & INT_MAX)


def _inproj_kernel(x_ref, g_ref, b_ref, w_ref, z_ref, zb_ref, xs_ref):
    @pl.when(pl.program_id(1) == 0)
    def _():
        xs_ref[...] = _layer_norm(x_ref[...], g_ref[...], b_ref[...]).astype(BF16)

    acc = _nn(xs_ref[...], w_ref[...])
    z_ref[...] = acc
    zb_ref[...] = acc.astype(BF16)


def _inproj(x_all, g, b, w, tm, tn):
    r = x_all.shape[0]
    return pl.pallas_call(
        _inproj_kernel,
        grid=(r // tm, ZW // tn),
        in_specs=[
            pl.BlockSpec((tm, D_MODEL), lambda i, n: (i, 0)),
            pl.BlockSpec((1, D_MODEL), lambda i, n: (0, 0)),
            pl.BlockSpec((1, D_MODEL), lambda i, n: (0, 0)),
            pl.BlockSpec((D_MODEL, tn), lambda i, n: (0, n)),
        ],
        out_specs=[
            pl.BlockSpec((tm, tn), lambda i, n: (i, n)),
            pl.BlockSpec((tm, tn), lambda i, n: (i, n)),
        ],
        out_shape=[jax.ShapeDtypeStruct((r, ZW), F32), jax.ShapeDtypeStruct((r, ZW), BF16)],
        scratch_shapes=[pltpu.VMEM((tm, D_MODEL), BF16)],
        compiler_params=pltpu.CompilerParams(
            dimension_semantics=("arbitrary", "arbitrary"), vmem_limit_bytes=VMEM_LIMIT),
        name="inproj",
    )(x_all, g, b, w)


KCH = 3
KC = KCH * BLK
GROUP = N_HEADS // N_KV


def _fold(a):
    out = a[:, 0:BLK]
    for i in range(1, KCH):
        out = out + a[:, BLK * i:BLK * (i + 1)]
    return out


def _tile3(a):
    return jnp.concatenate([a] * KCH, axis=1)


def _prompt_attn_kernel(q_ref, qi_ref, wi_ref, k_ref, v_ref, kk_ref, o_ref,
                        key_s, nd_s, lg_s, wb_s, qm_s, cut_s, acc_s, qs_s, m_s):
    j = pl.program_id(1)
    nch = (j + KCH) // KCH
    row = lax.broadcasted_iota(I32, (BLK, KC), 0)
    col = lax.broadcasted_iota(I32, (BLK, KC), 1)
    qrow = j * BLK + row
    zeros = jnp.zeros((BLK, BLK), F32)

    wi = wi_ref[...] * (IDX_SCALE * IDX_W_SCALE)
    qi = qi_ref[...]
    lane = lax.broadcasted_iota(I32, (BLK, BLK), 1)
    for h in range(N_HEADS):
        wb_s[h] = jnp.broadcast_to(wi[:, h:h + 1], (BLK, BLK))
        q2 = qi[:, BLK * (h // 2):BLK * (h // 2 + 1)]
        keep = (lane < IDX_DIM) if h % 2 == 0 else (lane >= IDX_DIM)
        qm_s[BLK * h:BLK * (h + 1), :] = jnp.where(keep, q2, jnp.zeros_like(q2))

    def score_body(c, carry):
        k0 = pl.multiple_of(c * KC, BLK)
        sh = _nt(qm_s[...], kk_ref[pl.ds(k0, KC), :])
        s = jnp.zeros((BLK, KC), F32)
        for h in range(N_HEADS):
            s = s + jnp.maximum(sh[BLK * h:BLK * (h + 1)], 0.0) * _tile3(wb_s[h])
        krow = k0 + col
        adm = (krow >= N_PAD) & (krow <= qrow)
        key_s[c] = jnp.where(adm, _sort_key(s), INT_MIN)
        return carry

    lax.fori_loop(0, nch, score_body, 0)

    def count(pred):
        body = lambda c, a: a + _fold(jnp.where(pred(c, key_s[c]), 1.0, 0.0))
        return jnp.sum(lax.fori_loop(0, nch, body, zeros), axis=-1, keepdims=True)

    def radix_body(i, p):
        cand = p + lax.shift_left(jnp.int32(1), 31 - i)
        candb = jnp.broadcast_to(cand, (BLK, KC))
        return jnp.where(count(lambda c, key: key >= candb) >= TOPK, cand, p)

    p = lax.fori_loop(0, 32, radix_body, jnp.full((BLK, 1), INT_MIN, I32))
    pb = jnp.broadcast_to(p, (BLK, KC))
    n_gt = count(lambda c, key: key > pb)
    n_ge = count(lambda c, key: key >= pb)
    need = TOPK - n_gt
    overflow = (n_ge > TOPK) & (p > INT_MIN)

    cut_s[...] = jnp.full((BLK, KC), INT_MAX, I32)

    @pl.when(jnp.max(jnp.where(overflow, 1.0, 0.0)) > 0.0)
    def _():
        def tie_body(i, x):
            cand = x + lax.shift_left(jnp.int32(1), 12 - i)
            candb = jnp.broadcast_to(cand, (BLK, KC))
            g = count(lambda c, key: (key == pb) & (c * KC + col < candb))
            return jnp.where(g < need, cand, x)

        x = lax.fori_loop(0, 13, tie_body, jnp.zeros((BLK, 1), I32))
        cut_s[...] = jnp.broadcast_to(jnp.where(overflow, x, INT_MAX), (BLK, KC))

    cutb = cut_s[...]

    def nd_body(c, carry):
        key = key_s[c]
        krow = c * KC + col
        sel = ((key > pb) | ((key == pb) & (krow <= cutb))) & (key > INT_MIN)
        nd_s[c] = jnp.where(sel, -(qrow - krow).astype(F32), -jnp.inf)
        return carry

    lax.fori_loop(0, nch, nd_body, 0)

    for h in range(N_HEADS):
        qs_s[BLK * h:BLK * (h + 1), :] = q_ref[:, HEAD_DIM * h:HEAD_DIM * (h + 1)]
    m_s[...] = jnp.full_like(m_s, -jnp.inf)
    acc_s[...] = jnp.zeros_like(acc_s)
    gr = GROUP * BLK

    def logit_body(c, carry):
        k0 = pl.multiple_of(c * KC, BLK)
        nd = nd_s[c]
        for g in range(N_KV):
            qk = _nt(qs_s[gr * g:gr * (g + 1), :], k_ref[pl.ds(k0, KC), HEAD_DIM * g:HEAD_DIM * (g + 1)])
            for i in range(GROUP):
                h = GROUP * g + i
                lg = qk[BLK * i:BLK * (i + 1)] * ATTN_SCALE + 2.0 ** -(h + 1) * nd
                lg_s[c, BLK * h:BLK * (h + 1), :] = lg
                m = m_s[BLK * h:BLK * (h + 1), :]
                for t in range(KCH):
                    m = jnp.maximum(m, lg[:, BLK * t:BLK * (t + 1)])
                m_s[BLK * h:BLK * (h + 1), :] = m
        return carry

    lax.fori_loop(0, nch, logit_body, 0)
    m = jnp.max(m_s[...], axis=-1, keepdims=True)
    m_s[...] = jnp.broadcast_to(jnp.where(m == -jnp.inf, 0.0, m), m_s.shape)

    ones = jnp.ones((KC, HEAD_DIM), BF16)

    def pv_body(c, carry):
        k0 = pl.multiple_of(c * KC, BLK)
        for g in range(N_KV):
            rows = slice(gr * g, gr * (g + 1))
            pr = jnp.exp(lg_s[c, rows, :] - _tile3(m_s[rows, :])).astype(BF16)
            v1 = jnp.concatenate([v_ref[pl.ds(k0, KC), HEAD_DIM * g:HEAD_DIM * (g + 1)], ones], axis=1)
            acc_s[rows, :] += _nn(pr, v1)
        return carry

    lax.fori_loop(0, nch, pv_body, 0)
    for h in range(N_HEADS):
        acc = acc_s[BLK * h:BLK * (h + 1), :]
        l = acc[:, HEAD_DIM:2 * HEAD_DIM]
        o_ref[:, HEAD_DIM * h:HEAD_DIM * (h + 1)] = (
            acc[:, 0:HEAD_DIM] / jnp.where(l == 0.0, 1.0, l)).astype(BF16)


def _prompt_attn(z, zb, nbatch, tp):
    nb = tp // BLK
    assert nb % KCH == 0
    nc = nb // KCH
    return pl.pallas_call(
        _prompt_attn_kernel,
        grid=(nbatch, nb),
        in_specs=[
            pl.BlockSpec((BLK, 1024), lambda b, j: (b * nb + j, C_Q // 1024)),
            pl.BlockSpec((BLK, 512), lambda b, j: (b * nb + j, C_QI // 512)),
            pl.BlockSpec((BLK, 128), lambda b, j: (b * nb + j, C_WI // 128)),
            pl.BlockSpec((tp, 512), lambda b, j: (b, C_K // 512)),
            pl.BlockSpec((tp, 512), lambda b, j: (b, C_V // 512)),
            pl.BlockSpec((tp, 128), lambda b, j: (b, C_KK // 128)),
        ],
        out_specs=pl.BlockSpec((BLK, 1024), lambda b, j: (b * nb + j, 0)),
        out_shape=jax.ShapeDtypeStruct((nbatch * tp, 1024), BF16),
        scratch_shapes=[
            pltpu.VMEM((nc, BLK, KC), I32),
            pltpu.VMEM((nc, BLK, KC), F32),
            pltpu.VMEM((nc, N_HEADS * BLK, KC), F32),
            pltpu.VMEM((N_HEADS, BLK, BLK), F32),
            pltpu.VMEM((N_HEADS * BLK, BLK), BF16),
            pltpu.VMEM((BLK, KC), I32),
            pltpu.VMEM((N_HEADS * BLK, 2 * HEAD_DIM), F32),
            pltpu.VMEM((N_HEADS * BLK, HEAD_DIM), BF16),
            pltpu.VMEM((N_HEADS * BLK, BLK), F32),
        ],
        compiler_params=pltpu.CompilerParams(
            dimension_semantics=("arbitrary", "arbitrary"), vmem_limit_bytes=VMEM_LIMIT),
        name="prompt_attn",
    )(zb, zb, z, zb, zb, zb)


def _scan_matrix():
    t = np.tril(np.ones((BLK, BLK), np.float32))
    mats = [t]
    for lvl in range(1, 8):
        bs = BLK >> (lvl - 1)
        r = np.arange(BLK)
        anchor = (r // bs) * bs + bs // 2 - 1
        mats.append(t[anchor])
    return np.concatenate(mats, axis=0)


def _prompt_hgrn_kernel(hq_ref, hf_ref, hi_ref, hg_ref, lbl_ref, ng_ref, scan_ref,
                        o_ref, s_ref, st_s):
    c = pl.program_id(1)
    nc = pl.num_programs(1)

    @pl.when(c == 0)
    def _():
        st_s[...] = jnp.zeros_like(st_s)

    row = lax.broadcasted_iota(I32, (BLK, BLK), 0)
    col = lax.broadcasted_iota(I32, (BLK, BLK), 1)
    valid = (c > 0) | (row >= N_PAD)
    lower = col < row
    x = row ^ col
    ll0, ll1 = lbl_ref[0:1, :], lbl_ref[1:2, :]
    mx = jnp.maximum(ll0, ll1)
    e0, e1 = jnp.exp(ll0 - mx), jnp.exp(ll1 - mx)
    lb_all = e0 / (e0 + e1)
    scan = scan_ref[...]
    ng = ng_ref[...]

    for h in range(N_HEADS):
        sl = slice(HEAD_DIM * h, HEAD_DIM * (h + 1))
        lb = lb_all[:, sl]
        f = lb + (1.0 - lb) * jax.nn.sigmoid(hf_ref[:, sl])
        lf = jnp.where(valid, jnp.log(f), 0.0)
        kf = jnp.where(valid, 1.0 - f, 0.0)
        hq = hq_ref[:, sl]
        qh = hq * jax.nn.sigmoid(hq)
        ih = hi_ref[:, sl].astype(BF16)

        p0 = lf.astype(BF16)
        r1 = lf - p0.astype(F32)
        p1 = r1.astype(BF16)
        p2 = (r1 - p1.astype(F32)).astype(BF16)
        ba = _nn(scan, jnp.concatenate([p0, p1, p2], axis=1))
        ba = ba[:, 0:BLK] + ba[:, BLK:2 * BLK] + ba[:, 2 * BLK:3 * BLK]
        b = ba[0:BLK]

        a = jnp.where(row == col, _nt(qh.astype(BF16), kf.astype(BF16)), 0.0)
        for lvl in range(1, 8):
            dec = jnp.exp(-jnp.abs(b - ba[BLK * lvl:BLK * (lvl + 1)]))
            qk = _nt((qh * dec).astype(BF16), (kf * dec).astype(BF16))
            a = a + jnp.where(lower & ((x >> (7 - lvl)) == 1), qk, 0.0)

        st = st_s[h]
        o = _nn(a.astype(BF16), ih) + _nt((qh * jnp.exp(b)).astype(BF16), st.astype(BF16))
        bend = b[BLK - 1:BLK, :]
        kend = (kf * jnp.exp(bend - b)).astype(BF16)
        st_new = st * jnp.exp(bend) + _nn(hi_ref[:, sl].T.astype(BF16), kend)
        st_s[h] = st_new

        on = o * lax.rsqrt(jnp.mean(o * o, axis=-1, keepdims=True) + NORM_EPS) * ng
        hg = hg_ref[:, sl]
        o_ref[:, sl] = (on * (hg * jax.nn.sigmoid(hg))).astype(BF16)

        @pl.when(c == nc - 1)
        def _():
            s_ref[0, h] = st_new.T


def _prompt_hgrn(z, lb_logits, norm_g, nbatch, tp):
    nb = tp // BLK
    scan = jnp.asarray(_scan_matrix(), BF16)
    row_spec = lambda cb: pl.BlockSpec((BLK, 1024), lambda b, c: (b * nb + c, cb))
    return pl.pallas_call(
        _prompt_hgrn_kernel,
        grid=(nbatch, nb),
        in_specs=[
            row_spec(C_HQ // 1024), row_spec(C_HF // 1024), row_spec(C_HI // 1024), row_spec(C_HG // 1024),
            pl.BlockSpec((2, 1024), lambda b, c: (0, 0)),
            pl.BlockSpec((1, HEAD_DIM), lambda b, c: (0, 0)),
            pl.BlockSpec((8 * BLK, BLK), lambda b, c: (0, 0)),
        ],
        out_specs=[
            pl.BlockSpec((BLK, 1024), lambda b, c: (b * nb + c, 0)),
            pl.BlockSpec((1, N_HEADS, HEAD_DIM, HEAD_DIM), lambda b, c: (b, 0, 0, 0)),
        ],
        out_shape=[
            jax.ShapeDtypeStruct((nbatch * tp, 1024), BF16),
            jax.ShapeDtypeStruct((nbatch, N_HEADS, HEAD_DIM, HEAD_DIM), F32),
        ],
        scratch_shapes=[pltpu.VMEM((N_HEADS, HEAD_DIM, HEAD_DIM), F32)],
        compiler_params=pltpu.CompilerParams(
            dimension_semantics=("arbitrary", "arbitrary"), vmem_limit_bytes=VMEM_LIMIT),
        name="prompt_hgrn",
    )(z, z, z, z, lb_logits, norm_g, scan)


PG = 8


def _dec_idx_proj_kernel(x_ref, g_ref, b_ref, w_ref, o_ref):
    xn = _layer_norm(x_ref[...], g_ref[...], b_ref[...])
    o_ref[...] = jnp.dot(xn, w_ref[...], preferred_element_type=F32, precision=lax.Precision.HIGHEST)


def _dec_idx_proj(x_s, g, b, w_idx):
    return pl.pallas_call(
        _dec_idx_proj_kernel,
        out_shape=jax.ShapeDtypeStruct((x_s.shape[0], w_idx.shape[1]), F32),
        compiler_params=pltpu.CompilerParams(vmem_limit_bytes=VMEM_LIMIT),
        name="dec_idx_proj",
    )(x_s, g, b, w_idx)


PGS = 16


def _dec_score_kernel(pt_ref, qi_ref, w_ref, *refs):
    kp_refs, out_ref = refs[:PGS], refs[PGS]
    qi = qi_ref[0]
    w = w_ref[0] * (IDX_SCALE * IDX_W_SCALE)
    for i in range(PGS):
        qk = jnp.dot(qi, kp_refs[i][0, 0], preferred_element_type=F32, precision=lax.Precision.HIGHEST)
        out_ref[0, i] = jnp.sum(jnp.maximum(qk, 0.0) * w, axis=0, keepdims=True)


def _dec_scores(page_table, qi_s, wcol, kidx_t):
    nbatch, npages = page_table.shape
    page_spec = lambda i: pl.BlockSpec(
        (1, 1, IDX_DIM, PAGE), lambda b, p, pt: (0, pt[b, p * PGS + i], 0, 0))
    return pl.pallas_call(
        _dec_score_kernel,
        grid_spec=pltpu.PrefetchScalarGridSpec(
            num_scalar_prefetch=1,
            grid=(nbatch, npages // PGS),
            in_specs=[
                pl.BlockSpec((1, N_HEADS, IDX_DIM), lambda b, p, pt: (b, 0, 0)),
                pl.BlockSpec((1, N_HEADS, 1), lambda b, p, pt: (b, 0, 0)),
            ] + [page_spec(i) for i in range(PGS)],
            out_specs=pl.BlockSpec((1, PGS, 1, PAGE), lambda b, p, pt: (b, p, 0, 0)),
        ),
        out_shape=jax.ShapeDtypeStruct((nbatch, npages, 1, PAGE), F32),
        compiler_params=pltpu.CompilerParams(dimension_semantics=("arbitrary", "arbitrary")),
        name="dec_scores",
    )(page_table, qi_s, wcol, *([kidx_t] * PGS))


def _dec_select_kernel(sc_ref, qi_ref, kin_ref, w_ref, nd_ref, ndn_ref, key_s):
    nq, npast = sc_ref.shape
    col = lax.broadcasted_iota(I32, (nq, npast), 1)
    key_s[...] = _sort_key(sc_ref[...])

    kin = kin_ref[...]
    w = w_ref[...] * (IDX_SCALE * IDX_W_SCALE)
    snew = jnp.zeros((nq, 1), F32)
    for h in range(N_HEADS):
        sh = jnp.sum(qi_ref[h] * kin, axis=-1, keepdims=True)
        snew = snew + jnp.maximum(sh, 0.0) * w[:, h:h + 1]
    keyn = _sort_key(snew)

    def count(pred_past, pred_new):
        c = jnp.sum(jnp.where(pred_past(key_s[...]), 1.0, 0.0), axis=-1, keepdims=True)
        return c + jnp.where(pred_new(keyn), 1.0, 0.0)

    def radix_body(i, p):
        cand = p + lax.shift_left(jnp.int32(1), 31 - i)
        return jnp.where(count(lambda k: k >= cand, lambda k: k >= cand) >= TOPK, cand, p)

    p = lax.fori_loop(0, 32, radix_body, jnp.full((nq, 1), INT_MIN, I32))
    n_gt = count(lambda k: k > p, lambda k: k > p)
    n_ge = count(lambda k: k >= p, lambda k: k >= p)
    need = TOPK - n_gt
    overflow = n_ge > TOPK

    def tie_body(i, x):
        cand = x + lax.shift_left(jnp.int32(1), 13 - i)
        g = count(lambda k: (k == p) & (col < cand), lambda k: (k == p) & (npast < cand))
        return jnp.where(g < need, cand, x)

    x = lax.fori_loop(0, 14, tie_body, jnp.zeros((nq, 1), I32))
    cut = jnp.where(overflow, x, INT_MAX)

    key = key_s[...]
    sel = (key > p) | ((key == p) & (col <= cut))
    nd_ref[...] = jnp.where(sel, -(npast - col).astype(F32), -jnp.inf)
    seln = (keyn > p) | ((keyn == p) & (npast <= cut))
    lane = lax.broadcasted_iota(I32, ndn_ref.shape, 1)
    ndn_ref[...] = jnp.where((lane == 0) & seln, 0.0, -jnp.inf)


def _dec_select(sc, qi_hm, ki_new, wrow):
    nq, npast = sc.shape
    return pl.pallas_call(
        _dec_select_kernel,
        out_shape=[jax.ShapeDtypeStruct((nq, npast), F32), jax.ShapeDtypeStruct((nq, PAGE), F32)],
        scratch_shapes=[pltpu.VMEM((nq, npast), I32)],
        compiler_params=pltpu.CompilerParams(vmem_limit_bytes=VMEM_LIMIT),
        name="dec_select",
    )(sc, qi_hm, ki_new, wrow)


def _dec_attn_kernel(pt_ref, q_ref, nd_ref, ndn_ref, kn_ref, vn_ref, *refs):
    k_refs, v_refs = refs[:PG], refs[PG:2 * PG]
    o_ref, m_s, l_s, acc_s = refs[2 * PG:]
    p = pl.program_id(1)
    hrow = lax.broadcasted_iota(I32, (N_HEADS, PAGE * N_KV), 0)
    ccol = lax.broadcasted_iota(I32, (N_HEADS, PAGE * N_KV), 1)
    own = (ccol % N_KV) == (hrow // GROUP)
    slope = jnp.exp2(-(hrow + 1).astype(F32))
    q = q_ref[0]

    def update(lgs, pv):
        m_old = m_s[:, 0:1]
        mx = lgs[0]
        for t in lgs[1:]:
            mx = jnp.maximum(mx, t)
        m_new = jnp.maximum(m_old, jnp.max(mx, axis=-1, keepdims=True))
        m_safe = jnp.where(m_new == -jnp.inf, 0.0, m_new)
        alpha = jnp.exp(m_old - m_safe)
        ps = [jnp.exp(t - m_safe) for t in lgs]
        ls = ps[0]
        for t in ps[1:]:
            ls = ls + t
        l_s[...] = alpha * l_s[...] + jnp.sum(ls, axis=-1, keepdims=True)
        acc_s[...] = alpha * acc_s[...] + pv(ps)
        m_s[...] = jnp.broadcast_to(m_new, m_s.shape)

    @pl.when(p == 0)
    def _():
        m_s[...] = jnp.full_like(m_s, -jnp.inf)
        l_s[...] = jnp.zeros_like(l_s)
        acc_s[...] = jnp.zeros_like(acc_s)
        ln = jnp.sum(q.astype(F32) * kn_ref[0].astype(F32), axis=-1, keepdims=True) * ATTN_SCALE
        lg = jnp.broadcast_to(ln, (N_HEADS, PAGE)) + ndn_ref[0]
        vn = vn_ref[0].astype(F32)
        update([lg], lambda ps: jnp.sum(ps[0], axis=-1, keepdims=True) * vn)

    lgs = []
    for i in range(PG):
        qk = _nt(q, k_refs[i][0].astype(BF16))
        lgs.append(jnp.where(own, qk * ATTN_SCALE + slope * nd_ref[0, i], -jnp.inf))

    def pv(ps):
        out = jnp.zeros((N_HEADS, HEAD_DIM), F32)
        for i in range(PG):
            out = out + _nn(ps[i].astype(BF16), v_refs[i][0].astype(BF16))
        return out

    update(lgs, pv)

    @pl.when(p == pl.num_programs(1) - 1)
    def _():
        l = l_s[...]
        o_ref[0] = (acc_s[...] / jnp.where(l == 0.0, 1.0, l)).astype(BF16)


def _dec_attn(page_table, q8, nd4, ndn, kn8, vn8, k_flat, v_flat):
    nbatch, npages = page_table.shape
    page_spec = lambda i: pl.BlockSpec(
        (1, PAGE * N_KV, HEAD_DIM), lambda b, p, pt: (pt[b, p * PG + i], 0, 0))
    head_spec = pl.BlockSpec((1, N_HEADS, HEAD_DIM), lambda b, p, pt: (b, 0, 0))
    return pl.pallas_call(
        _dec_attn_kernel,
        grid_spec=pltpu.PrefetchScalarGridSpec(
            num_scalar_prefetch=1,
            grid=(nbatch, npages // PG),
            in_specs=[
                head_spec,
                pl.BlockSpec((1, PG, 1, PAGE * N_KV), lambda b, p, pt: (b, p, 0, 0)),
                pl.BlockSpec((1, 1, PAGE), lambda b, p, pt: (b, 0, 0)),
                head_spec, head_spec,
            ] + [page_spec(i) for i in range(PG)] + [page_spec(i) for i in range(PG)],
            out_specs=head_spec,
            scratch_shapes=[pltpu.VMEM((N_HEADS, PAGE), F32)] * 3,
        ),
        out_shape=jax.ShapeDtypeStruct((nbatch, N_HEADS, HEAD_DIM), BF16),
        compiler_params=pltpu.CompilerParams(
            dimension_semantics=("arbitrary", "arbitrary"), vmem_limit_bytes=VMEM_LIMIT),
        name="dec_attn",
    )(page_table, q8, nd4, ndn, kn8, vn8, *([k_flat] * PG), *([v_flat] * PG))


def _dec_hgrn_kernel(s_ref, hq_ref, hf_ref, hi_ref, hg_ref, lbl_ref, ng_ref, so_ref, o_ref):
    ll0, ll1 = lbl_ref[0], lbl_ref[1]
    mx = jnp.maximum(ll0, ll1)
    e0, e1 = jnp.exp(ll0 - mx), jnp.exp(ll1 - mx)
    lb = e0 / (e0 + e1)
    f = lb + (1.0 - lb) * jax.nn.sigmoid(hf_ref[0])
    hq = hq_ref[0]
    q = hq * jax.nn.sigmoid(hq)
    ng = ng_ref[...]
    for h in range(N_HEADS):
        s_new = s_ref[0, 0, h] * f[h] + (1.0 - f[h]) * hi_ref[0, h]
        so_ref[0, h] = s_new
        o = jnp.sum(s_new * q[h], axis=0, keepdims=True)
        on = o * lax.rsqrt(jnp.mean(o * o, axis=-1, keepdims=True) + NORM_EPS) * ng
        hg = hg_ref[0, h]
        o_ref[0, h] = (on * (hg * jax.nn.sigmoid(hg))).astype(BF16)


def _dec_hgrn(state, hq_c, hf_c, hi_r, hg_r, lbl_c, norm_g):
    nbatch = state.shape[1]
    col_spec = pl.BlockSpec((1, N_HEADS, HEAD_DIM, 1), lambda b: (b, 0, 0, 0))
    row_spec = pl.BlockSpec((1, N_HEADS, 1, HEAD_DIM), lambda b: (b, 0, 0, 0))
    return pl.pallas_call(
        _dec_hgrn_kernel,
        grid=(nbatch,),
        in_specs=[
            pl.BlockSpec((1, 1, N_HEADS, HEAD_DIM, HEAD_DIM), lambda b: (0, b, 0, 0, 0)),
            col_spec, col_spec, row_spec, row_spec,
            pl.BlockSpec((2, N_HEADS, HEAD_DIM, 1), lambda b: (0, 0, 0, 0)),
            pl.BlockSpec((1, HEAD_DIM), lambda b: (0, 0)),
        ],
        out_specs=[
            pl.BlockSpec((1, N_HEADS, HEAD_DIM, HEAD_DIM), lambda b: (b, 0, 0, 0)),
            row_spec,
        ],
        out_shape=[
            jax.ShapeDtypeStruct((nbatch, N_HEADS, HEAD_DIM, HEAD_DIM), F32),
            jax.ShapeDtypeStruct((nbatch, N_HEADS, 1, HEAD_DIM), BF16),
        ],
        compiler_params=pltpu.CompilerParams(dimension_semantics=("arbitrary",)),
        name="dec_hgrn",
    )(state, hq_c, hf_c, hi_r, hg_r, lbl_c, norm_g)


def _outproj_kernel(x_ref, gi_ref, bi_ref, a_ref, h_ref, wa_ref, wh_ref, g_ref, b_ref, o_ref):
    xn = _layer_norm(x_ref[...], gi_ref[...], bi_ref[...])
    mix = _nn(a_ref[...], wa_ref[...]) + _nn(h_ref[...], wh_ref[...])
    o_ref[...] = _layer_norm(ALPHA * xn + mix, g_ref[...], b_ref[...])


def _outproj(x_all, gi, bi, attn, hgm, wa, wh, g, b, tm):
    r = x_all.shape[0]
    vec = pl.BlockSpec((1, D_MODEL), lambda i: (0, 0))
    return pl.pallas_call(
        _outproj_kernel,
        grid=(r // tm,),
        in_specs=[
            pl.BlockSpec((tm, D_MODEL), lambda i: (i, 0)), vec, vec,
            pl.BlockSpec((tm, 1024), lambda i: (i, 0)),
            pl.BlockSpec((tm, 1024), lambda i: (i, 0)),
            pl.BlockSpec((1024, D_MODEL), lambda i: (0, 0)),
            pl.BlockSpec((1024, D_MODEL), lambda i: (0, 0)),
            vec, vec,
        ],
        out_specs=pl.BlockSpec((tm, D_MODEL), lambda i: (i, 0)),
        out_shape=jax.ShapeDtypeStruct((r, D_MODEL), F32),
        compiler_params=pltpu.CompilerParams(
            dimension_semantics=("arbitrary",), vmem_limit_bytes=VMEM_LIMIT),
        name="outproj",
    )(x_all, gi, bi, attn, hgm, wa, wh, g, b)


def _ffn_kernel(x_ref, wg_ref, wu_ref, wd_ref, g_ref, b_ref, y_ref, xb_ref):
    f = pl.program_id(1)

    @pl.when(f == 0)
    def _():
        xb_ref[...] = x_ref[...].astype(BF16)
        y_ref[...] = jnp.zeros_like(y_ref)

    xb = xb_ref[...]
    gate = _nn(xb, wg_ref[...])
    up = _nn(xb, wu_ref[...])
    hid = (gate * jax.nn.sigmoid(gate) * up).astype(BF16)
    y_ref[...] += _nn(hid, wd_ref[...])

    @pl.when(f == pl.num_programs(1) - 1)
    def _():
        y_ref[...] = _layer_norm(ALPHA * x_ref[...] + y_ref[...], g_ref[...], b_ref[...])


def _ffn(x1, wg, wu, wd, g, b, tm, tf):
    r = x1.shape[0]
    dff = wg.shape[1]
    vec = pl.BlockSpec((1, D_MODEL), lambda i, f: (0, 0))
    return pl.pallas_call(
        _ffn_kernel,
        grid=(r // tm, dff // tf),
        in_specs=[
            pl.BlockSpec((tm, D_MODEL), lambda i, f: (i, 0)),
            pl.BlockSpec((D_MODEL, tf), lambda i, f: (0, f)),
            pl.BlockSpec((D_MODEL, tf), lambda i, f: (0, f)),
            pl.BlockSpec((tf, D_MODEL), lambda i, f: (f, 0)),
            vec, vec,
        ],
        out_specs=pl.BlockSpec((tm, D_MODEL), lambda i, f: (i, 0)),
        out_shape=jax.ShapeDtypeStruct((r, D_MODEL), F32),
        scratch_shapes=[pltpu.VMEM((tm, D_MODEL), BF16)],
        compiler_params=pltpu.CompilerParams(
            dimension_semantics=("arbitrary", "arbitrary"), vmem_limit_bytes=VMEM_LIMIT),
        name="ffn",
    )(x1, wg, wu, wd, g, b)


def _reorder_w_in(w):
    q, k, v, qi, wi, ki, hq, hf, hi, hg = jnp.split(
        w, [1024, 1536, 2048, 2560, 2568, 2632, 3656, 4680, 5704], axis=1)
    pad = jnp.zeros((w.shape[0], 128 - wi.shape[1]), w.dtype)
    return jnp.concatenate([q, hq, hf, hi, hg, k, v, qi, ki, ki, wi, pad], axis=1).astype(BF16)


def kernel(x_prompt, x_sample, cache_k, cache_v, cache_kidx, state_hgrn, page_table, meta_tokens,
           ln_in_g, ln_in_b, w_in, hg_lb_logits, hg_norm_g, w_out, ln1_g, ln1_b, w_gate, w_up,
           w_down, ln2_g, ln2_b):
    nbatch, seq, _ = x_prompt.shape
    ndec = x_sample.shape[0]
    assert w_in.shape[0] == 1 and x_sample.shape[1] == 1 and seq % BLK == 0
    tp = BLK + seq
    rp = nbatch * tp
    ntail = DEC_ROWS - ndec
    r = rp + DEC_ROWS
    assert ntail >= 0 and r % (20 * 16) == 0

    head = jnp.concatenate([jnp.zeros((N_PAD, D_MODEL), F32), meta_tokens.astype(F32)], axis=0)
    parts = []
    for b in range(nbatch):
        parts += [head, x_prompt[b]]
    x_all = jnp.concatenate(
        parts + [x_sample.reshape(ndec, D_MODEL), jnp.zeros((ntail, D_MODEL), F32)], axis=0)

    row = lambda a: a.reshape(1, -1).astype(F32)
    tm = r // 10
    z, zb = _inproj(x_all, row(ln_in_g), row(ln_in_b), _reorder_w_in(w_in[0]), tm, 1152)

    lbl = hg_lb_logits.astype(F32)
    ng = row(hg_norm_g[0])
    attn_p = _prompt_attn(z, zb, nbatch, tp)
    hgm_p, s_p = _prompt_hgrn(z, lbl, ng, nbatch, tp)

    zs, zbs = z[rp:rp + ndec], zb[rp:rp + ndec]
    npast = page_table.shape[1] * PAGE
    wi0 = w_in[0]
    w_idx = jnp.concatenate([wi0[:, 2048:2560], wi0[:, 2568:2632], wi0[:, 2560:2568],
                             jnp.zeros((D_MODEL, 56), F32)], axis=1)
    z_idx = _dec_idx_proj(x_sample.reshape(ndec, D_MODEL), row(ln_in_g), row(ln_in_b), w_idx)
    qi_s = z_idx[:, 0:512].reshape(ndec, N_HEADS, IDX_DIM)
    wrow = z_idx[:, 576:576 + N_HEADS]
    kidx_t = jnp.swapaxes(cache_kidx, 2, 3)
    k_flat = cache_k[0].reshape(-1, PAGE * N_KV, HEAD_DIM)
    v_flat = cache_v[0].reshape(-1, PAGE * N_KV, HEAD_DIM)
    sc = _dec_scores(page_table, qi_s, wrow[:, :, None], kidx_t).reshape(ndec, npast)
    nd, ndn = _dec_select(sc, jnp.swapaxes(qi_s, 0, 1), z_idx[:, 512:576], wrow)
    nd4 = jnp.repeat(nd, N_KV, axis=1).reshape(ndec, npast // PAGE, 1, PAGE * N_KV)
    rep = lambda a: jnp.repeat(a.reshape(ndec, N_KV, HEAD_DIM), N_HEADS // N_KV, axis=1)
    attn_s = _dec_attn(page_table, zbs[:, C_Q:C_Q + 1024].reshape(ndec, N_HEADS, HEAD_DIM),
                       nd4, ndn.reshape(ndec, 1, PAGE),
                       rep(zbs[:, C_K:C_K + 512]), rep(zbs[:, C_V:C_V + 512]), k_flat, v_flat)
    colv = lambda c0: zs[:, c0:c0 + 1024].reshape(ndec, N_HEADS, HEAD_DIM, 1)
    rowv = lambda c0: zs[:, c0:c0 + 1024].reshape(ndec, N_HEADS, 1, HEAD_DIM)
    s_s, hgm_s = _dec_hgrn(state_hgrn, colv(C_HQ), colv(C_HF), rowv(C_HI), rowv(C_HG),
                           lbl.reshape(2, N_HEADS, HEAD_DIM, 1), ng)

    tail = jnp.zeros((ntail, 1024), BF16)
    attn = jnp.concatenate([attn_p, attn_s.reshape(ndec, 1024), tail], axis=0)
    hgm = jnp.concatenate([hgm_p, hgm_s.reshape(ndec, 1024), tail], axis=0)
    wo = w_out[0].astype(BF16)
    x1 = _outproj(x_all, row(ln_in_g), row(ln_in_b), attn, hgm, wo[:1024], wo[1024:],
                  row(ln1_g[0]), row(ln1_b[0]), r // 20)
    y = _ffn(x1, w_gate[0].astype(BF16), w_up[0].astype(BF16), w_down[0].astype(BF16),
             row(ln2_g[0]), row(ln2_b[0]), tm, 512)

    def prompt_cols(a, c0, width, first_row):
        return jnp.stack([lax.slice(a, (b * tp + first_row, c0), ((b + 1) * tp, c0 + width))
                          for b in range(nbatch)])

    y_prompt = prompt_cols(y, 0, D_MODEL, BLK)
    y_sample = y[rp:rp + ndec].reshape(ndec, 1, D_MODEL)
    k_p = prompt_cols(z, C_K, 512, N_PAD).reshape(1, nbatch, tp - N_PAD, N_KV, HEAD_DIM)
    v_p = prompt_cols(z, C_V, 512, N_PAD).reshape(1, nbatch, tp - N_PAD, N_KV, HEAD_DIM)
    ki_p = prompt_cols(z, C_KK, IDX_DIM, N_PAD)[None]
    k_s = zs[:, C_K:C_K + 512].reshape(1, ndec, 1, N_KV, HEAD_DIM)
    v_s = zs[:, C_V:C_V + 512].reshape(1, ndec, 1, N_KV, HEAD_DIM)
    ki_s = zs[:, C_KK:C_KK + IDX_DIM].reshape(1, ndec, 1, IDX_DIM)
    return (y_prompt, y_sample, k_p, v_p, ki_p, s_p[None], k_s, v_s, ki_s, s_s[None])
```

```python
import functools

import numpy as np
import jax
import jax.numpy as jnp
from jax import lax
from jax.experimental import pallas as pl
from jax.experimental.pallas import tpu as pltpu

F32 = jnp.float32
BF16 = jnp.bfloat16
I32 = jnp.int32

D_MODEL = 2048
N_META = 16
BLK = 128
N_PAD = BLK - N_META
N_HEADS = 8
N_KV = 4
HEAD_DIM = 128
IDX_DIM = 64
TOPK = 256
PAGE = 128
LN_EPS = 1e-5
NORM_EPS = 1e-6
ALPHA = 2.0 ** 0.25
ATTN_SCALE = HEAD_DIM ** -0.5
IDX_SCALE = IDX_DIM ** -0.5
IDX_W_SCALE = N_HEADS ** -0.5
INT_MIN = -2 ** 31
INT_MAX = 2 ** 31 - 1
DEC_ROWS = 192

C_Q, C_HQ, C_HF, C_HI, C_HG = 0, 1024, 2048, 3072, 4096
C_K, C_V, C_QI, C_KK, C_WI = 5120, 5632, 6144, 6656, 6784
ZW = 6912
C32 = 4608

VMEM_LIMIT = 56 * 1024 * 1024


def _nt(a, b):
    return lax.dot_general(a, b, (((1,), (1,)), ((), ())), preferred_element_type=F32)


def _nn(a, b):
    return jnp.dot(a, b, preferred_element_type=F32)


def _layer_norm(x, g, b):
    mu = jnp.mean(x, axis=-1, keepdims=True)
    xc = x - mu
    var = jnp.mean(xc * xc, axis=-1, keepdims=True)
    return xc * lax.rsqrt(var + LN_EPS) * g + b


def _sort_key(s):
    s = jnp.where(s == 0.0, 0.0, s)
    bits = pltpu.bitcast(s, I32)
    return bits ^ ((bits >> 31) & INT_MAX)


def _inproj_kernel(x_ref, g_ref, b_ref, w_ref, z_ref, zb_ref, xs_ref, *, first_f32_tile):
    n = pl.program_id(1)

    @pl.when(n == 0)
    def _():
        xs_ref[...] = _layer_norm(x_ref[...], g_ref[...], b_ref[...]).astype(BF16)

    acc = _nn(xs_ref[...], w_ref[...])
    zb_ref[...] = acc.astype(BF16)

    @pl.when(n >= first_f32_tile)
    def _():
        z_ref[...] = acc


def _inproj(x_all, g, b, w, tm, tn):
    r = x_all.shape[0]
    assert C32 % tn == 0 and ZW % tn == 0
    n0 = C32 // tn
    return pl.pallas_call(
        functools.partial(_inproj_kernel, first_f32_tile=n0),
        grid=(r // tm, ZW // tn),
        in_specs=[
            pl.BlockSpec((tm, D_MODEL), lambda i, n: (i, 0)),
            pl.BlockSpec((1, D_MODEL), lambda i, n: (0, 0)),
            pl.BlockSpec((1, D_MODEL), lambda i, n: (0, 0)),
            pl.BlockSpec((D_MODEL, tn), lambda i, n: (0, n)),
        ],
        out_specs=[
            pl.BlockSpec((tm, tn), lambda i, n: (i, jnp.maximum(n - n0, 0))),
            pl.BlockSpec((tm, tn), lambda i, n: (i, n)),
        ],
        out_shape=[jax.ShapeDtypeStruct((r, ZW - C32), F32), jax.ShapeDtypeStruct((r, ZW), BF16)],
        scratch_shapes=[pltpu.VMEM((tm, D_MODEL), BF16)],
        compiler_params=pltpu.CompilerParams(
            dimension_semantics=("arbitrary", "arbitrary"), vmem_limit_bytes=VMEM_LIMIT),
        name="inproj",
    )(x_all, g, b, w)


KCH = 3
KC = KCH * BLK
GROUP = N_HEADS // N_KV


def _fold(a):
    out = a[:, 0:BLK]
    for i in range(1, KCH):
        out = out + a[:, BLK * i:BLK * (i + 1)]
    return out


def _tile3(a):
    return jnp.concatenate([a] * KCH, axis=1)


def _prompt_attn_kernel(q_ref, qi_ref, wi_ref, k_ref, v_ref, kk_ref, o_ref,
                        key_s, nd_s, lg_s, wb_s, qm_s, cut_s, acc_s, qs_s, m_s):
    j = pl.program_id(1)
    nch = (j + KCH) // KCH
    row = lax.broadcasted_iota(I32, (BLK, KC), 0)
    col = lax.broadcasted_iota(I32, (BLK, KC), 1)
    qrow = j * BLK + row
    zeros = jnp.zeros((BLK, BLK), F32)

    wi = wi_ref[...] * (IDX_SCALE * IDX_W_SCALE)
    qi = qi_ref[...]
    lane = lax.broadcasted_iota(I32, (BLK, BLK), 1)
    for h in range(N_HEADS):
        wb_s[h] = jnp.broadcast_to(wi[:, h:h + 1], (BLK, BLK))
        q2 = qi[:, BLK * (h // 2):BLK * (h // 2 + 1)]
        keep = (lane < IDX_DIM) if h % 2 == 0 else (lane >= IDX_DIM)
        qm_s[BLK * h:BLK * (h + 1), :] = jnp.where(keep, q2, jnp.zeros_like(q2))

    def score_body(c, carry):
        k0 = pl.multiple_of(c * KC, BLK)
        sh = _nt(qm_s[...], kk_ref[pl.ds(k0, KC), :])
        s = jnp.zeros((BLK, KC), F32)
        for h in range(N_HEADS):
            s = s + jnp.maximum(sh[BLK * h:BLK * (h + 1)], 0.0) * _tile3(wb_s[h])
        krow = k0 + col
        adm = (krow >= N_PAD) & (krow <= qrow)
        key_s[c] = jnp.where(adm, _sort_key(s), INT_MIN)
        return carry

    lax.fori_loop(0, nch, score_body, 0)

    def count(pred):
        body = lambda c, a: a + _fold(jnp.where(pred(c, key_s[c]), 1.0, 0.0))
        return jnp.sum(lax.fori_loop(0, nch, body, zeros), axis=-1, keepdims=True)

    def radix_body(i, p):
        cand = p + lax.shift_left(jnp.int32(1), 31 - i)
        candb = jnp.broadcast_to(cand, (BLK, KC))
        return jnp.where(count(lambda c, key: key >= candb) >= TOPK, cand, p)

    p = lax.fori_loop(0, 32, radix_body, jnp.full((BLK, 1), INT_MIN, I32))
    pb = jnp.broadcast_to(p, (BLK, KC))
    n_gt = count(lambda c, key: key > pb)
    n_ge = count(lambda c, key: key >= pb)
    need = TOPK - n_gt
    overflow = (n_ge > TOPK) & (p > INT_MIN)

    cut_s[...] = jnp.full((BLK, KC), INT_MAX, I32)

    @pl.when(jnp.max(jnp.where(overflow, 1.0, 0.0)) > 0.0)
    def _():
        def tie_body(i, x):
            cand = x + lax.shift_left(jnp.int32(1), 12 - i)
            candb = jnp.broadcast_to(cand, (BLK, KC))
            g = count(lambda c, key: (key == pb) & (c * KC + col < candb))
            return jnp.where(g < need, cand, x)

        x = lax.fori_loop(0, 13, tie_body, jnp.zeros((BLK, 1), I32))
        cut_s[...] = jnp.broadcast_to(jnp.where(overflow, x, INT_MAX), (BLK, KC))

    cutb = cut_s[...]

    def nd_body(c, carry):
        key = key_s[c]
        krow = c * KC + col
        sel = ((key > pb) | ((key == pb) & (krow <= cutb))) & (key > INT_MIN)
        nd_s[c] = jnp.where(sel, -(qrow - krow).astype(F32), -jnp.inf)
        return carry

    lax.fori_loop(0, nch, nd_body, 0)

    for h in range(N_HEADS):
        qs_s[BLK * h:BLK * (h + 1), :] = q_ref[:, HEAD_DIM * h:HEAD_DIM * (h + 1)]
    m_s[...] = jnp.full_like(m_s, -jnp.inf)
    acc_s[...] = jnp.zeros_like(acc_s)
    gr = GROUP * BLK

    def logit_body(c, carry):
        k0 = pl.multiple_of(c * KC, BLK)
        nd = nd_s[c]
        for g in range(N_KV):
            qk = _nt(qs_s[gr * g:gr * (g + 1), :], k_ref[pl.ds(k0, KC), HEAD_DIM * g:HEAD_DIM * (g + 1)])
            for i in range(GROUP):
                h = GROUP * g + i
                lg = qk[BLK * i:BLK * (i + 1)] * ATTN_SCALE + 2.0 ** -(h + 1) * nd
                lg_s[c, BLK * h:BLK * (h + 1), :] = lg
                m = m_s[BLK * h:BLK * (h + 1), :]
                for t in range(KCH):
                    m = jnp.maximum(m, lg[:, BLK * t:BLK * (t + 1)])
                m_s[BLK * h:BLK * (h + 1), :] = m
        return carry

    lax.fori_loop(0, nch, logit_body, 0)
    m = jnp.max(m_s[...], axis=-1, keepdims=True)
    m_s[...] = jnp.broadcast_to(jnp.where(m == -jnp.inf, 0.0, m), m_s.shape)

    ones = jnp.ones((KC, HEAD_DIM), BF16)

    def pv_body(c, carry):
        k0 = pl.multiple_of(c * KC, BLK)
        for g in range(N_KV):
            rows = slice(gr * g, gr * (g + 1))
            pr = jnp.exp(lg_s[c, rows, :] - _tile3(m_s[rows, :])).astype(BF16)
            v1 = jnp.concatenate([v_ref[pl.ds(k0, KC), HEAD_DIM * g:HEAD_DIM * (g + 1)], ones], axis=1)
            acc_s[rows, :] += _nn(pr, v1)
        return carry

    lax.fori_loop(0, nch, pv_body, 0)
    for h in range(N_HEADS):
        acc = acc_s[BLK * h:BLK * (h + 1), :]
        l = acc[:, HEAD_DIM:2 * HEAD_DIM]
        o_ref[:, HEAD_DIM * h:HEAD_DIM * (h + 1)] = (
            acc[:, 0:HEAD_DIM] / jnp.where(l == 0.0, 1.0, l)).astype(BF16)


def _prompt_attn(z32, zb, nbatch, tp):
    nb = tp // BLK
    assert nb % KCH == 0
    nc = nb // KCH
    return pl.pallas_call(
        _prompt_attn_kernel,
        grid=(nbatch, nb),
        in_specs=[
            pl.BlockSpec((BLK, 1024), lambda b, j: (b * nb + j, C_Q // 1024)),
            pl.BlockSpec((BLK, 512), lambda b, j: (b * nb + j, C_QI // 512)),
            pl.BlockSpec((BLK, 128), lambda b, j: (b * nb + j, (C_WI - C32) // 128)),
            pl.BlockSpec((tp, 512), lambda b, j: (b, C_K // 512)),
            pl.BlockSpec((tp, 512), lambda b, j: (b, C_V // 512)),
            pl.BlockSpec((tp, 128), lambda b, j: (b, C_KK // 128)),
        ],
        out_specs=pl.BlockSpec((BLK, 1024), lambda b, j: (b * nb + j, 0)),
        out_shape=jax.ShapeDtypeStruct((nbatch * tp, 1024), BF16),
        scratch_shapes=[
            pltpu.VMEM((nc, BLK, KC), I32),
            pltpu.VMEM((nc, BLK, KC), F32),
            pltpu.VMEM((nc, N_HEADS * BLK, KC), F32),
            pltpu.VMEM((N_HEADS, BLK, BLK), F32),
            pltpu.VMEM((N_HEADS * BLK, BLK), BF16),
            pltpu.VMEM((BLK, KC), I32),
            pltpu.VMEM((N_HEADS * BLK, 2 * HEAD_DIM), F32),
            pltpu.VMEM((N_HEADS * BLK, HEAD_DIM), BF16),
            pltpu.VMEM((N_HEADS * BLK, BLK), F32),
        ],
        compiler_params=pltpu.CompilerParams(
            dimension_semantics=("arbitrary", "arbitrary"), vmem_limit_bytes=VMEM_LIMIT),
        name="prompt_attn",
    )(zb, zb, z32, zb, zb, zb)


def _scan_matrix():
    t = np.tril(np.ones((BLK, BLK), np.float32))
    mats = [t]
    for lvl in range(1, 8):
        bs = BLK >> (lvl - 1)
        r = np.arange(BLK)
        anchor = (r // bs) * bs + bs // 2 - 1
        mats.append(t[anchor])
    return np.concatenate(mats, axis=0)


def _level_signs():
    r = np.arange(BLK)
    rows = [np.where((r >> (7 - lvl)) & 1, 1.0, -1.0) for lvl in range(1, 8)]
    return np.repeat(np.concatenate(rows)[:, None], BLK, axis=1).astype(np.float32)


def _prompt_hgrn_kernel(hq_ref, hf_ref, hi_ref, hg_ref, lbl_ref, ng_ref, scan_ref, sg_ref,
                        o_ref, s_ref, st_s):
    c = pl.program_id(1)
    nc = pl.num_programs(1)

    @pl.when(c == 0)
    def _():
        st_s[...] = jnp.zeros_like(st_s)

    row = lax.broadcasted_iota(I32, (BLK, BLK), 0)
    col = lax.broadcasted_iota(I32, (BLK, BLK), 1)
    valid = (c > 0) | (row >= N_PAD)
    lower = col < row
    x = row ^ col
    ll0, ll1 = lbl_ref[0:1, :], lbl_ref[1:2, :]
    mx = jnp.maximum(ll0, ll1)
    e0, e1 = jnp.exp(ll0 - mx), jnp.exp(ll1 - mx)
    lb_all = e0 / (e0 + e1)
    ng = ng_ref[...]

    for h in range(N_HEADS):
        sl = slice(HEAD_DIM * h, HEAD_DIM * (h + 1))
        lb = lb_all[:, sl]
        f = lb + (1.0 - lb) * jax.nn.sigmoid(hf_ref[:, sl].astype(F32))
        lf = jnp.where(valid, jnp.log(f), 0.0)
        kf = jnp.where(valid, 1.0 - f, 0.0)
        hq = hq_ref[:, sl].astype(F32)
        qh = hq * jax.nn.sigmoid(hq)
        ih = hi_ref[:, sl]

        p0 = lf.astype(BF16)
        r1 = lf - p0.astype(F32)
        p1 = r1.astype(BF16)
        p2 = (r1 - p1.astype(F32)).astype(BF16)
        b3 = _nn(scan_ref[0:BLK, :], jnp.concatenate([p0, p1, p2], axis=1))
        b = b3[:, 0:BLK] + b3[:, BLK:2 * BLK] + b3[:, 2 * BLK:3 * BLK]
        anc = _nn(scan_ref[BLK:, :], p0)

        a = jnp.where(row == col, _nt(qh.astype(BF16), kf.astype(BF16)), 0.0)
        for lvl in range(1, 8):
            rows = slice(BLK * (lvl - 1), BLK * lvl)
            dec = jnp.exp((b - anc[rows]) * sg_ref[rows, :])
            qk = _nt((qh * dec).astype(BF16), (kf * dec).astype(BF16))
            a = a + jnp.where(lower & ((x >> (7 - lvl)) == 1), qk, 0.0)

        st = st_s[h]
        o = _nn(a.astype(BF16), ih) + _nt((qh * jnp.exp(b)).astype(BF16), st.astype(BF16))
        bend = b[BLK - 1:BLK, :]
        kend = (kf * jnp.exp(bend - b)).astype(BF16)
        st_new = st * jnp.exp(bend) + _nn(ih.astype(F32).T.astype(BF16), kend)
        st_s[h] = st_new

        on = o * lax.rsqrt(jnp.mean(o * o, axis=-1, keepdims=True) + NORM_EPS) * ng
        hg = hg_ref[:, sl].astype(F32)
        o_ref[:, sl] = (on * (hg * jax.nn.sigmoid(hg))).astype(BF16)

        @pl.when(c == nc - 1)
        def _():
            s_ref[0, h] = st_new.T


def _prompt_hgrn(zb, lb_logits, norm_g, nbatch, tp):
    nb = tp // BLK
    scan = jnp.asarray(_scan_matrix(), BF16)
    signs = jnp.asarray(_level_signs(), F32)
    row_spec = lambda cb: pl.BlockSpec((BLK, 1024), lambda b, c: (b * nb + c, cb))
    return pl.pallas_call(
        _prompt_hgrn_kernel,
        grid=(nbatch, nb),
        in_specs=[
            row_spec(C_HQ // 1024), row_spec(C_HF // 1024), row_spec(C_HI // 1024), row_spec(C_HG // 1024),
            pl.BlockSpec((2, 1024), lambda b, c: (0, 0)),
            pl.BlockSpec((1, HEAD_DIM), lambda b, c: (0, 0)),
            pl.BlockSpec((8 * BLK, BLK), lambda b, c: (0, 0)),
            pl.BlockSpec((7 * BLK, BLK), lambda b, c: (0, 0)),
        ],
        out_specs=[
            pl.BlockSpec((BLK, 1024), lambda b, c: (b * nb + c, 0)),
            pl.BlockSpec((1, N_HEADS, HEAD_DIM, HEAD_DIM), lambda b, c: (b, 0, 0, 0)),
        ],
        out_shape=[
            jax.ShapeDtypeStruct((nbatch * tp, 1024), BF16),
            jax.ShapeDtypeStruct((nbatch, N_HEADS, HEAD_DIM, HEAD_DIM), F32),
        ],
        scratch_shapes=[pltpu.VMEM((N_HEADS, HEAD_DIM, HEAD_DIM), F32)],
        compiler_params=pltpu.CompilerParams(
            dimension_semantics=("arbitrary", "arbitrary"), vmem_limit_bytes=VMEM_LIMIT),
        name="prompt_hgrn",
    )(zb, zb, zb, zb, lb_logits, norm_g, scan, signs)


PG = 16


def _dec_idx_proj_kernel(x_ref, g_ref, b_ref, w_ref, o_ref):
    xn = _layer_norm(x_ref[...], g_ref[...], b_ref[...])
    o_ref[...] = jnp.dot(xn, w_ref[...], preferred_element_type=F32, precision=lax.Precision.HIGHEST)


def _dec_idx_proj(x_s, g, b, w_idx):
    return pl.pallas_call(
        _dec_idx_proj_kernel,
        out_shape=jax.ShapeDtypeStruct((x_s.shape[0], w_idx.shape[1]), F32),
        compiler_params=pltpu.CompilerParams(vmem_limit_bytes=VMEM_LIMIT),
        name="dec_idx_proj",
    )(x_s, g, b, w_idx)


PGS = 32


def _dec_score_kernel(pt_ref, qi_ref, w_ref, *refs):
    kp_refs, out_ref = refs[:PGS], refs[PGS]
    qi = qi_ref[0]
    w = w_ref[0] * (IDX_SCALE * IDX_W_SCALE)
    for i in range(PGS):
        qk = jnp.dot(qi, kp_refs[i][0, 0], preferred_element_type=F32, precision=lax.Precision.HIGHEST)
        out_ref[0, i] = jnp.sum(jnp.maximum(qk, 0.0) * w, axis=0, keepdims=True)


def _dec_scores(page_table, qi_s, wcol, kidx_t):
    nbatch, npages = page_table.shape
    page_spec = lambda i: pl.BlockSpec(
        (1, 1, IDX_DIM, PAGE), lambda b, p, pt: (0, pt[b, p * PGS + i], 0, 0))
    return pl.pallas_call(
        _dec_score_kernel,
        grid_spec=pltpu.PrefetchScalarGridSpec(
            num_scalar_prefetch=1,
            grid=(nbatch, npages // PGS),
            in_specs=[
                pl.BlockSpec((1, N_HEADS, IDX_DIM), lambda b, p, pt: (b, 0, 0)),
                pl.BlockSpec((1, N_HEADS, 1), lambda b, p, pt: (b, 0, 0)),
            ] + [page_spec(i) for i in range(PGS)],
            out_specs=pl.BlockSpec((1, PGS, 1, PAGE), lambda b, p, pt: (b, p, 0, 0)),
        ),
        out_shape=jax.ShapeDtypeStruct((nbatch, npages, 1, PAGE), F32),
        compiler_params=pltpu.CompilerParams(dimension_semantics=("arbitrary", "arbitrary")),
        name="dec_scores",
    )(page_table, qi_s, wcol, *([kidx_t] * PGS))


def _dec_select_kernel(sc_ref, qi_ref, kin_ref, w_ref, nd_ref, ndn_ref, key_s):
    nq, npast = sc_ref.shape
    col = lax.broadcasted_iota(I32, (nq, npast), 1)
    key_s[...] = _sort_key(sc_ref[...])

    kin = kin_ref[...]
    w = w_ref[...] * (IDX_SCALE * IDX_W_SCALE)
    snew = jnp.zeros((nq, 1), F32)
    for h in range(N_HEADS):
        sh = jnp.sum(qi_ref[h] * kin, axis=-1, keepdims=True)
        snew = snew + jnp.maximum(sh, 0.0) * w[:, h:h + 1]
    keyn = _sort_key(snew)

    def count(pred_past, pred_new):
        c = jnp.sum(jnp.where(pred_past(key_s[...]), 1.0, 0.0), axis=-1, keepdims=True)
        return c + jnp.where(pred_new(keyn), 1.0, 0.0)

    def radix_body(i, p):
        cand = p + lax.shift_left(jnp.int32(1), 31 - i)
        return jnp.where(count(lambda k: k >= cand, lambda k: k >= cand) >= TOPK, cand, p)

    p = lax.fori_loop(0, 32, radix_body, jnp.full((nq, 1), INT_MIN, I32))
    n_gt = count(lambda k: k > p, lambda k: k > p)
    n_ge = count(lambda k: k >= p, lambda k: k >= p)
    need = TOPK - n_gt
    overflow = n_ge > TOPK

    def tie_body(i, x):
        cand = x + lax.shift_left(jnp.int32(1), 13 - i)
        g = count(lambda k: (k == p) & (col < cand), lambda k: (k == p) & (npast < cand))
        return jnp.where(g < need, cand, x)

    x = lax.fori_loop(0, 14, tie_body, jnp.zeros((nq, 1), I32))
    cut = jnp.where(overflow, x, INT_MAX)

    key = key_s[...]
    sel = (key > p) | ((key == p) & (col <= cut))
    nd_ref[...] = jnp.where(sel, -(npast - col).astype(F32), -jnp.inf)
    seln = (keyn > p) | ((keyn == p) & (npast <= cut))
    lane = lax.broadcasted_iota(I32, ndn_ref.shape, 1)
    ndn_ref[...] = jnp.where((lane == 0) & seln, 0.0, -jnp.inf)


def _dec_select(sc, qi_hm, ki_new, wrow):
    nq, npast = sc.shape
    return pl.pallas_call(
        _dec_select_kernel,
        out_shape=[jax.ShapeDtypeStruct((nq, npast), F32), jax.ShapeDtypeStruct((nq, PAGE), F32)],
        scratch_shapes=[pltpu.VMEM((nq, npast), I32)],
        compiler_params=pltpu.CompilerParams(vmem_limit_bytes=VMEM_LIMIT),
        name="dec_select",
    )(sc, qi_hm, ki_new, wrow)


def _dec_attn_kernel(pt_ref, q_ref, nd_ref, ndn_ref, kn_ref, vn_ref, *refs):
    k_refs, v_refs = refs[:PG], refs[PG:2 * PG]
    o_ref, m_s, l_s, acc_s = refs[2 * PG:]
    p = pl.program_id(1)
    hrow = lax.broadcasted_iota(I32, (N_HEADS, PAGE * N_KV), 0)
    ccol = lax.broadcasted_iota(I32, (N_HEADS, PAGE * N_KV), 1)
    own = (ccol % N_KV) == (hrow // GROUP)
    slope = jnp.exp2(-(hrow + 1).astype(F32))
    q = q_ref[0]

    def update(lgs, pv):
        m_old = m_s[:, 0:1]
        mx = lgs[0]
        for t in lgs[1:]:
            mx = jnp.maximum(mx, t)
        m_new = jnp.maximum(m_old, jnp.max(mx, axis=-1, keepdims=True))
        m_safe = jnp.where(m_new == -jnp.inf, 0.0, m_new)
        alpha = jnp.exp(m_old - m_safe)
        ps = [jnp.exp(t - m_safe) for t in lgs]
        ls = ps[0]
        for t in ps[1:]:
            ls = ls + t
        l_s[...] = alpha * l_s[...] + jnp.sum(ls, axis=-1, keepdims=True)
        acc_s[...] = alpha * acc_s[...] + pv(ps)
        m_s[...] = jnp.broadcast_to(m_new, m_s.shape)

    @pl.when(p == 0)
    def _():
        m_s[...] = jnp.full_like(m_s, -jnp.inf)
        l_s[...] = jnp.zeros_like(l_s)
        acc_s[...] = jnp.zeros_like(acc_s)
        ln = jnp.sum(q.astype(F32) * kn_ref[0].astype(F32), axis=-1, keepdims=True) * ATTN_SCALE
        lg = jnp.broadcast_to(ln, (N_HEADS, PAGE)) + ndn_ref[0]
        vn = vn_ref[0].astype(F32)
        update([lg], lambda ps: jnp.sum(ps[0], axis=-1, keepdims=True) * vn)

    lgs = []
    for i in range(PG):
        qk = _nt(q, k_refs[i][0].astype(BF16))
        lgs.append(jnp.where(own, qk * ATTN_SCALE + slope * nd_ref[0, i], -jnp.inf))

    def pv(ps):
        out = jnp.zeros((N_HEADS, HEAD_DIM), F32)
        for i in range(PG):
            out = out + _nn(ps[i].astype(BF16), v_refs[i][0].astype(BF16))
        return out

    update(lgs, pv)

    @pl.when(p == pl.num_programs(1) - 1)
    def _():
        l = l_s[...]
        o_ref[0] = (acc_s[...] / jnp.where(l == 0.0, 1.0, l)).astype(BF16)


def _dec_attn(page_table, q8, nd4, ndn, kn8, vn8, k_flat, v_flat):
    nbatch, npages = page_table.shape
    page_spec = lambda i: pl.BlockSpec(
        (1, PAGE * N_KV, HEAD_DIM), lambda b, p, pt: (pt[b, p * PG + i], 0, 0))
    head_spec = pl.BlockSpec((1, N_HEADS, HEAD_DIM), lambda b, p, pt: (b, 0, 0))
    return pl.pallas_call(
        _dec_attn_kernel,
        grid_spec=pltpu.PrefetchScalarGridSpec(
            num_scalar_prefetch=1,
            grid=(nbatch, npages // PG),
            in_specs=[
                head_spec,
                pl.BlockSpec((1, PG, 1, PAGE * N_KV), lambda b, p, pt: (b, p, 0, 0)),
                pl.BlockSpec((1, 1, PAGE), lambda b, p, pt: (b, 0, 0)),
                head_spec, head_spec,
            ] + [page_spec(i) for i in range(PG)] + [page_spec(i) for i in range(PG)],
            out_specs=head_spec,
            scratch_shapes=[pltpu.VMEM((N_HEADS, PAGE), F32)] * 3,
        ),
        out_shape=jax.ShapeDtypeStruct((nbatch, N_HEADS, HEAD_DIM), BF16),
        compiler_params=pltpu.CompilerParams(
            dimension_semantics=("arbitrary", "arbitrary"), vmem_limit_bytes=VMEM_LIMIT),
        name="dec_attn",
    )(page_table, q8, nd4, ndn, kn8, vn8, *([k_flat] * PG), *([v_flat] * PG))


def _dec_hgrn_kernel(s_ref, hq_ref, hf_ref, hi_ref, hg_ref, lbl_ref, ng_ref, so_ref, o_ref):
    ll0, ll1 = lbl_ref[0], lbl_ref[1]
    mx = jnp.maximum(ll0, ll1)
    e0, e1 = jnp.exp(ll0 - mx), jnp.exp(ll1 - mx)
    lb = e0 / (e0 + e1)
    f = lb + (1.0 - lb) * jax.nn.sigmoid(hf_ref[0])
    hq = hq_ref[0]
    q = hq * jax.nn.sigmoid(hq)
    ng = ng_ref[...]
    for h in range(N_HEADS):
        s_new = s_ref[0, 0, h] * f[h] + (1.0 - f[h]) * hi_ref[0, h]
        so_ref[0, h] = s_new
        o = jnp.sum(s_new * q[h], axis=0, keepdims=True)
        on = o * lax.rsqrt(jnp.mean(o * o, axis=-1, keepdims=True) + NORM_EPS) * ng
        hg = hg_ref[0, h]
        o_ref[0, h] = (on * (hg * jax.nn.sigmoid(hg))).astype(BF16)


def _dec_hgrn(state, hq_c, hf_c, hi_r, hg_r, lbl_c, norm_g):
    nbatch = state.shape[1]
    col_spec = pl.BlockSpec((1, N_HEADS, HEAD_DIM, 1), lambda b: (b, 0, 0, 0))
    row_spec = pl.BlockSpec((1, N_HEADS, 1, HEAD_DIM), lambda b: (b, 0, 0, 0))
    return pl.pallas_call(
        _dec_hgrn_kernel,
        grid=(nbatch,),
        in_specs=[
            pl.BlockSpec((1, 1, N_HEADS, HEAD_DIM, HEAD_DIM), lambda b: (0, b, 0, 0, 0)),
            col_spec, col_spec, row_spec, row_spec,
            pl.BlockSpec((2, N_HEADS, HEAD_DIM, 1), lambda b: (0, 0, 0, 0)),
            pl.BlockSpec((1, HEAD_DIM), lambda b: (0, 0)),
        ],
        out_specs=[
            pl.BlockSpec((1, N_HEADS, HEAD_DIM, HEAD_DIM), lambda b: (b, 0, 0, 0)),
            row_spec,
        ],
        out_shape=[
            jax.ShapeDtypeStruct((nbatch, N_HEADS, HEAD_DIM, HEAD_DIM), F32),
            jax.ShapeDtypeStruct((nbatch, N_HEADS, 1, HEAD_DIM), BF16),
        ],
        compiler_params=pltpu.CompilerParams(dimension_semantics=("arbitrary",)),
        name="dec_hgrn",
    )(state, hq_c, hf_c, hi_r, hg_r, lbl_c, norm_g)


def _outproj_kernel(x_ref, gi_ref, bi_ref, a_ref, h_ref, wa_ref, wh_ref, g_ref, b_ref, o_ref):
    xn = _layer_norm(x_ref[...], gi_ref[...], bi_ref[...])
    mix = _nn(a_ref[...], wa_ref[...]) + _nn(h_ref[...], wh_ref[...])
    o_ref[...] = _layer_norm(ALPHA * xn + mix, g_ref[...], b_ref[...])


def _outproj(x_all, gi, bi, attn, hgm, wa, wh, g, b, tm):
    r = x_all.shape[0]
    vec = pl.BlockSpec((1, D_MODEL), lambda i: (0, 0))
    return pl.pallas_call(
        _outproj_kernel,
        grid=(r // tm,),
        in_specs=[
            pl.BlockSpec((tm, D_MODEL), lambda i: (i, 0)), vec, vec,
            pl.BlockSpec((tm, 1024), lambda i: (i, 0)),
            pl.BlockSpec((tm, 1024), lambda i: (i, 0)),
            pl.BlockSpec((1024, D_MODEL), lambda i: (0, 0)),
            pl.BlockSpec((1024, D_MODEL), lambda i: (0, 0)),
            vec, vec,
        ],
        out_specs=pl.BlockSpec((tm, D_MODEL), lambda i: (i, 0)),
        out_shape=jax.ShapeDtypeStruct((r, D_MODEL), F32),
        compiler_params=pltpu.CompilerParams(
            dimension_semantics=("arbitrary",), vmem_limit_bytes=VMEM_LIMIT),
        name="outproj",
    )(x_all, gi, bi, attn, hgm, wa, wh, g, b)


def _ffn_kernel(x_ref, wg_ref, wu_ref, wd_ref, g_ref, b_ref, y_ref, xb_ref):
    f = pl.program_id(1)

    @pl.when(f == 0)
    def _():
        xb_ref[...] = x_ref[...].astype(BF16)
        y_ref[...] = jnp.zeros_like(y_ref)

    xb = xb_ref[...]
    gate = _nn(xb, wg_ref[...])
    up = _nn(xb, wu_ref[...])
    hid = (gate * jax.nn.sigmoid(gate) * up).astype(BF16)
    y_ref[...] += _nn(hid, wd_ref[...])

    @pl.when(f == pl.num_programs(1) - 1)
    def _():
        y_ref[...] = _layer_norm(ALPHA * x_ref[...] + y_ref[...], g_ref[...], b_ref[...])


def _ffn(x1, wg, wu, wd, g, b, tm, tf):
    r = x1.shape[0]
    dff = wg.shape[1]
    vec = pl.BlockSpec((1, D_MODEL), lambda i, f: (0, 0))
    return pl.pallas_call(
        _ffn_kernel,
        grid=(r // tm, dff // tf),
        in_specs=[
            pl.BlockSpec((tm, D_MODEL), lambda i, f: (i, 0)),
            pl.BlockSpec((D_MODEL, tf), lambda i, f: (0, f)),
            pl.BlockSpec((D_MODEL, tf), lambda i, f: (0, f)),
            pl.BlockSpec((tf, D_MODEL), lambda i, f: (f, 0)),
            vec, vec,
        ],
        out_specs=pl.BlockSpec((tm, D_MODEL), lambda i, f: (i, 0)),
        out_shape=jax.ShapeDtypeStruct((r, D_MODEL), F32),
        scratch_shapes=[pltpu.VMEM((tm, D_MODEL), BF16)],
        compiler_params=pltpu.CompilerParams(
            dimension_semantics=("arbitrary", "arbitrary"), vmem_limit_bytes=VMEM_LIMIT),
        name="ffn",
    )(x1, wg, wu, wd, g, b)


def _reorder_w_in(w):
    q, k, v, qi, wi, ki, hq, hf, hi, hg = jnp.split(
        w, [1024, 1536, 2048, 2560, 2568, 2632, 3656, 4680, 5704], axis=1)
    pad = jnp.zeros((w.shape[0], 128 - wi.shape[1]), w.dtype)
    return jnp.concatenate([q, hq, hf, hi, hg, k, v, qi, ki, ki, wi, pad], axis=1).astype(BF16)


def kernel(x_prompt, x_sample, cache_k, cache_v, cache_kidx, state_hgrn, page_table, meta_tokens,
           ln_in_g, ln_in_b, w_in, hg_lb_logits, hg_norm_g, w_out, ln1_g, ln1_b, w_gate, w_up,
           w_down, ln2_g, ln2_b):
    nbatch, seq, _ = x_prompt.shape
    ndec = x_sample.shape[0]
    assert w_in.shape[0] == 1 and x_sample.shape[1] == 1 and seq % BLK == 0
    tp = BLK + seq
    rp = nbatch * tp
    ntail = DEC_ROWS - ndec
    r = rp + DEC_ROWS
    assert ntail >= 0 and r % (20 * 16) == 0

    head = jnp.concatenate([jnp.zeros((N_PAD, D_MODEL), F32), meta_tokens.astype(F32)], axis=0)
    parts = []
    for b in range(nbatch):
        parts += [head, x_prompt[b]]
    x_all = jnp.concatenate(
        parts + [x_sample.reshape(ndec, D_MODEL), jnp.zeros((ntail, D_MODEL), F32)], axis=0)

    row = lambda a: a.reshape(1, -1).astype(F32)
    tm = r // 10
    z32, zb = _inproj(x_all, row(ln_in_g), row(ln_in_b), _reorder_w_in(w_in[0]), tm, 1152)

    lbl = hg_lb_logits.astype(F32)
    ng = row(hg_norm_g[0])
    attn_p = _prompt_attn(z32, zb, nbatch, tp)
    hgm_p, s_p = _prompt_hgrn(zb, lbl, ng, nbatch, tp)

    zs32, zbs = z32[rp:rp + ndec], zb[rp:rp + ndec]
    npast = page_table.shape[1] * PAGE
    wi0 = w_in[0]
    w_idx = jnp.concatenate([wi0[:, 2048:2560], wi0[:, 2568:2632], wi0[:, 2560:2568],
                             jnp.zeros((D_MODEL, 56), F32)], axis=1)
    z_idx = _dec_idx_proj(x_sample.reshape(ndec, D_MODEL), row(ln_in_g), row(ln_in_b), w_idx)
    qi_s = z_idx[:, 0:512].reshape(ndec, N_HEADS, IDX_DIM)
    wrow = z_idx[:, 576:576 + N_HEADS]
    kidx_t = jnp.swapaxes(cache_kidx, 2, 3)
    k_flat = cache_k[0].reshape(-1, PAGE * N_KV, HEAD_DIM)
    v_flat = cache_v[0].reshape(-1, PAGE * N_KV, HEAD_DIM)
    sc = _dec_scores(page_table, qi_s, wrow[:, :, None], kidx_t).reshape(ndec, npast)
    nd, ndn = _dec_select(sc, jnp.swapaxes(qi_s, 0, 1), z_idx[:, 512:576], wrow)
    nd4 = jnp.repeat(nd, N_KV, axis=1).reshape(ndec, npast // PAGE, 1, PAGE * N_KV)
    rep = lambda a: jnp.repeat(a.reshape(ndec, N_KV, HEAD_DIM), N_HEADS // N_KV, axis=1)
    attn_s = _dec_attn(page_table, zbs[:, C_Q:C_Q + 1024].reshape(ndec, N_HEADS, HEAD_DIM),
                       nd4, ndn.reshape(ndec, 1, PAGE),
                       rep(zbs[:, C_K:C_K + 512]), rep(zbs[:, C_V:C_V + 512]), k_flat, v_flat)
    colv = lambda c0: zbs[:, c0:c0 + 1024].astype(F32).reshape(ndec, N_HEADS, HEAD_DIM, 1)
    rowv = lambda c0: zbs[:, c0:c0 + 1024].astype(F32).reshape(ndec, N_HEADS, 1, HEAD_DIM)
    s_s, hgm_s = _dec_hgrn(state_hgrn, colv(C_HQ), colv(C_HF), rowv(C_HI), rowv(C_HG),
                           lbl.reshape(2, N_HEADS, HEAD_DIM, 1), ng)

    tail = jnp.zeros((ntail, 1024), BF16)
    attn = jnp.concatenate([attn_p, attn_s.reshape(ndec, 1024), tail], axis=0)
    hgm = jnp.concatenate([hgm_p, hgm_s.reshape(ndec, 1024), tail], axis=0)
    wo = w_out[0].astype(BF16)
    x1 = _outproj(x_all, row(ln_in_g), row(ln_in_b), attn, hgm, wo[:1024], wo[1024:],
                  row(ln1_g[0]), row(ln1_b[0]), r // 20)
    y = _ffn(x1, w_gate[0].astype(BF16), w_up[0].astype(BF16), w_down[0].astype(BF16),
             row(ln2_g[0]), row(ln2_b[0]), tm, 512)

    def prompt_cols(a, c0, width, first_row):
        return jnp.stack([lax.slice(a, (b * tp + first_row, c0), ((b + 1) * tp, c0 + width))
                          for b in range(nbatch)])

    y_prompt = prompt_cols(y, 0, D_MODEL, BLK)
    y_sample = y[rp:rp + ndec].reshape(ndec, 1, D_MODEL)
    ck, cv, ckk = C_K - C32, C_V - C32, C_KK - C32
    k_p = prompt_cols(z32, ck, 512, N_PAD).reshape(1, nbatch, tp - N_PAD, N_KV, HEAD_DIM)
    v_p = prompt_cols(z32, cv, 512, N_PAD).reshape(1, nbatch, tp - N_PAD, N_KV, HEAD_DIM)
    ki_p = prompt_cols(z32, ckk, IDX_DIM, N_PAD)[None]
    k_s = zs32[:, ck:ck + 512].reshape(1, ndec, 1, N_KV, HEAD_DIM)
    v_s = zs32[:, cv:cv + 512].reshape(1, ndec, 1, N_KV, HEAD_DIM)
    ki_s = zs32[:, ckk:ckk + IDX_DIM].reshape(1, ndec, 1, IDX_DIM)
    return (y_prompt, y_sample, k_p, v_p, ki_p, s_p[None], k_s, v_s, ki_s, s_s[None])
```

```python
import functools

import numpy as np
import jax
import jax.numpy as jnp
from jax import lax
from jax.experimental import pallas as pl
from jax.experimental.pallas import tpu as pltpu

F32 = jnp.float32
BF16 = jnp.bfloat16
I32 = jnp.int32

D_MODEL = 2048
N_META = 16
BLK = 128
N_PAD = BLK - N_META
N_HEADS = 8
N_KV = 4
HEAD_DIM = 128
IDX_DIM = 64
TOPK = 256
PAGE = 128
LN_EPS = 1e-5
NORM_EPS = 1e-6
ALPHA = 2.0 ** 0.25
ATTN_SCALE = HEAD_DIM ** -0.5
IDX_SCALE = IDX_DIM ** -0.5
IDX_W_SCALE = N_HEADS ** -0.5
INT_MIN = -2 ** 31
INT_MAX = 2 ** 31 - 1
DEC_ROWS = 192

C_Q, C_HQ, C_HF, C_HI, C_HG = 0, 1024, 2048, 3072, 4096
C_K, C_V, C_QI, C_KK, C_WI = 5120, 5632, 6144, 6656, 6784
ZW = 6912
C32 = 4608

VMEM_LIMIT = 56 * 1024 * 1024


def _nt(a, b):
    return lax.dot_general(a, b, (((1,), (1,)), ((), ())), preferred_element_type=F32)


def _nn(a, b):
    return jnp.dot(a, b, preferred_element_type=F32)


def _layer_norm(x, g, b):
    mu = jnp.mean(x, axis=-1, keepdims=True)
    xc = x - mu
    var = jnp.mean(xc * xc, axis=-1, keepdims=True)
    return xc * lax.rsqrt(var + LN_EPS) * g + b


KEY_NEG_INF = INT_MIN + 0x7FFFFF


def _key_to_float(k):
    return pltpu.bitcast(k ^ ((k >> 31) & INT_MAX), F32)


def _sort_key(s):
    s = jnp.where(s == 0.0, 0.0, s)
    bits = pltpu.bitcast(s, I32)
    return bits ^ ((bits >> 31) & INT_MAX)


def _inproj_kernel(x_ref, g_ref, b_ref, w_ref, z_ref, zb_ref, xs_ref, *, first_f32_tile):
    n = pl.program_id(1)

    @pl.when(n == 0)
    def _():
        xs_ref[...] = _layer_norm(x_ref[...], g_ref[...], b_ref[...]).astype(BF16)

    acc = _nn(xs_ref[...], w_ref[...])
    zb_ref[...] = acc.astype(BF16)

    @pl.when(n >= first_f32_tile)
    def _():
        z_ref[...] = acc


def _inproj(x_all, g, b, w, tm, tn):
    r = x_all.shape[0]
    assert C32 % tn == 0 and ZW % tn == 0
    n0 = C32 // tn
    return pl.pallas_call(
        functools.partial(_inproj_kernel, first_f32_tile=n0),
        grid=(r // tm, ZW // tn),
        in_specs=[
            pl.BlockSpec((tm, D_MODEL), lambda i, n: (i, 0)),
            pl.BlockSpec((1, D_MODEL), lambda i, n: (0, 0)),
            pl.BlockSpec((1, D_MODEL), lambda i, n: (0, 0)),
            pl.BlockSpec((D_MODEL, tn), lambda i, n: (0, n)),
        ],
        out_specs=[
            pl.BlockSpec((tm, tn), lambda i, n: (i, jnp.maximum(n - n0, 0))),
            pl.BlockSpec((tm, tn), lambda i, n: (i, n)),
        ],
        out_shape=[jax.ShapeDtypeStruct((r, ZW - C32), F32), jax.ShapeDtypeStruct((r, ZW), BF16)],
        scratch_shapes=[pltpu.VMEM((tm, D_MODEL), BF16)],
        compiler_params=pltpu.CompilerParams(
            dimension_semantics=("arbitrary", "arbitrary"), vmem_limit_bytes=VMEM_LIMIT),
        name="inproj",
    )(x_all, g, b, w)


KCH = 3
KC = KCH * BLK
GROUP = N_HEADS // N_KV


def _fold(a):
    out = a[:, 0:BLK]
    for i in range(1, KCH):
        out = out + a[:, BLK * i:BLK * (i + 1)]
    return out


def _tile3(a):
    return jnp.concatenate([a] * KCH, axis=1)


def _prompt_attn_kernel(q_ref, qi_ref, wi_ref, k_ref, v_ref, kk_ref, o_ref,
                        sc_s, nd_s, lg_s, wb_s, qm_s, cut_s, acc_s, qs_s, m_s):
    j = pl.program_id(1)
    nch = (j + KCH) // KCH
    row = lax.broadcasted_iota(I32, (BLK, KC), 0)
    col = lax.broadcasted_iota(I32, (BLK, KC), 1)
    qrow = j * BLK + row
    zeros = jnp.zeros((BLK, BLK), F32)

    wi = wi_ref[...] * (IDX_SCALE * IDX_W_SCALE)
    qi = qi_ref[...]
    lane = lax.broadcasted_iota(I32, (BLK, BLK), 1)
    for h in range(N_HEADS):
        wb_s[h] = jnp.broadcast_to(wi[:, h:h + 1], (BLK, BLK))
        q2 = qi[:, BLK * (h // 2):BLK * (h // 2 + 1)]
        keep = (lane < IDX_DIM) if h % 2 == 0 else (lane >= IDX_DIM)
        qm_s[BLK * h:BLK * (h + 1), :] = jnp.where(keep, q2, jnp.zeros_like(q2))

    def score_body(c, carry):
        k0 = pl.multiple_of(c * KC, BLK)
        sh = _nt(qm_s[...], kk_ref[pl.ds(k0, KC), :])
        s = jnp.zeros((BLK, KC), F32)
        for h in range(N_HEADS):
            s = s + jnp.maximum(sh[BLK * h:BLK * (h + 1)], 0.0) * _tile3(wb_s[h])
        krow = k0 + col
        adm = (krow >= N_PAD) & (krow <= qrow)
        sc_s[c] = jnp.where(adm, jnp.where(s == 0.0, 0.0, s), -jnp.inf)
        return carry

    lax.fori_loop(0, nch, score_body, 0)

    def count(pred):
        body = lambda c, a: a + _fold(jnp.where(pred(c, sc_s[c]), 1.0, 0.0))
        return jnp.sum(lax.fori_loop(0, nch, body, zeros), axis=-1, keepdims=True)

    def radix_body(i, p):
        cand = p + lax.shift_left(jnp.int32(1), 31 - i)
        candb = jnp.broadcast_to(_key_to_float(cand), (BLK, KC))
        return jnp.where(count(lambda c, s: s >= candb) >= TOPK, cand, p)

    p = lax.fori_loop(0, 32, radix_body, jnp.full((BLK, 1), INT_MIN, I32))
    p = jnp.maximum(p, KEY_NEG_INF)
    pf = _key_to_float(p)
    pb = jnp.broadcast_to(pf, (BLK, KC))
    n_gt = count(lambda c, s: s > pb)
    n_ge = count(lambda c, s: s >= pb)
    need = TOPK - n_gt
    overflow = (n_ge > TOPK) & (pf > -jnp.inf)

    cut_s[...] = jnp.full((BLK, KC), INT_MAX, I32)

    @pl.when(jnp.max(jnp.where(overflow, 1.0, 0.0)) > 0.0)
    def _():
        def tie_body(i, x):
            cand = x + lax.shift_left(jnp.int32(1), 12 - i)
            candb = jnp.broadcast_to(cand, (BLK, KC))
            g = count(lambda c, s: (s == pb) & (c * KC + col < candb))
            return jnp.where(g < need, cand, x)

        x = lax.fori_loop(0, 13, tie_body, jnp.zeros((BLK, 1), I32))
        cut_s[...] = jnp.broadcast_to(jnp.where(overflow, x, INT_MAX), (BLK, KC))

    cutb = cut_s[...]

    def nd_body(c, carry):
        s = sc_s[c]
        krow = c * KC + col
        sel = ((s > pb) | ((s == pb) & (krow <= cutb))) & (s > -jnp.inf)
        nd_s[c] = jnp.where(sel, -(qrow - krow).astype(F32), -jnp.inf)
        return carry

    lax.fori_loop(0, nch, nd_body, 0)

    for h in range(N_HEADS):
        qs_s[BLK * h:BLK * (h + 1), :] = q_ref[:, HEAD_DIM * h:HEAD_DIM * (h + 1)]
    m_s[...] = jnp.full_like(m_s, -jnp.inf)
    acc_s[...] = jnp.zeros_like(acc_s)
    gr = GROUP * BLK

    def logit_body(c, carry):
        k0 = pl.multiple_of(c * KC, BLK)
        nd = nd_s[c]
        for g in range(N_KV):
            qk = _nt(qs_s[gr * g:gr * (g + 1), :], k_ref[pl.ds(k0, KC), HEAD_DIM * g:HEAD_DIM * (g + 1)])
            for i in range(GROUP):
                h = GROUP * g + i
                lg = qk[BLK * i:BLK * (i + 1)] * ATTN_SCALE + 2.0 ** -(h + 1) * nd
                lg_s[c, BLK * h:BLK * (h + 1), :] = lg
                m = m_s[BLK * h:BLK * (h + 1), :]
                for t in range(KCH):
                    m = jnp.maximum(m, lg[:, BLK * t:BLK * (t + 1)])
                m_s[BLK * h:BLK * (h + 1), :] = m
        return carry

    lax.fori_loop(0, nch, logit_body, 0)
    m = jnp.max(m_s[...], axis=-1, keepdims=True)
    m_s[...] = jnp.broadcast_to(jnp.where(m == -jnp.inf, 0.0, m), m_s.shape)

    ones = jnp.ones((KC, HEAD_DIM), BF16)

    def pv_body(c, carry):
        k0 = pl.multiple_of(c * KC, BLK)
        for g in range(N_KV):
            rows = slice(gr * g, gr * (g + 1))
            pr = jnp.exp(lg_s[c, rows, :] - _tile3(m_s[rows, :])).astype(BF16)
            v1 = jnp.concatenate([v_ref[pl.ds(k0, KC), HEAD_DIM * g:HEAD_DIM * (g + 1)], ones], axis=1)
            acc_s[rows, :] += _nn(pr, v1)
        return carry

    lax.fori_loop(0, nch, pv_body, 0)
    for h in range(N_HEADS):
        acc = acc_s[BLK * h:BLK * (h + 1), :]
        l = acc[:, HEAD_DIM:2 * HEAD_DIM]
        o_ref[:, HEAD_DIM * h:HEAD_DIM * (h + 1)] = (
            acc[:, 0:HEAD_DIM] / jnp.where(l == 0.0, 1.0, l)).astype(BF16)


def _prompt_attn(z32, zb, nbatch, tp):
    nb = tp // BLK
    assert nb % KCH == 0
    nc = nb // KCH
    return pl.pallas_call(
        _prompt_attn_kernel,
        grid=(nbatch, nb),
        in_specs=[
            pl.BlockSpec((BLK, 1024), lambda b, j: (b * nb + j, C_Q // 1024)),
            pl.BlockSpec((BLK, 512), lambda b, j: (b * nb + j, C_QI // 512)),
            pl.BlockSpec((BLK, 128), lambda b, j: (b * nb + j, (C_WI - C32) // 128)),
            pl.BlockSpec((tp, 512), lambda b, j: (b, C_K // 512)),
            pl.BlockSpec((tp, 512), lambda b, j: (b, C_V // 512)),
            pl.BlockSpec((tp, 128), lambda b, j: (b, C_KK // 128)),
        ],
        out_specs=pl.BlockSpec((BLK, 1024), lambda b, j: (b * nb + j, 0)),
        out_shape=jax.ShapeDtypeStruct((nbatch * tp, 1024), BF16),
        scratch_shapes=[
            pltpu.VMEM((nc, BLK, KC), F32),
            pltpu.VMEM((nc, BLK, KC), F32),
            pltpu.VMEM((nc, N_HEADS * BLK, KC), F32),
            pltpu.VMEM((N_HEADS, BLK, BLK), F32),
            pltpu.VMEM((N_HEADS * BLK, BLK), BF16),
            pltpu.VMEM((BLK, KC), I32),
            pltpu.VMEM((N_HEADS * BLK, 2 * HEAD_DIM), F32),
            pltpu.VMEM((N_HEADS * BLK, HEAD_DIM), BF16),
            pltpu.VMEM((N_HEADS * BLK, BLK), F32),
        ],
        compiler_params=pltpu.CompilerParams(
            dimension_semantics=("arbitrary", "arbitrary"), vmem_limit_bytes=VMEM_LIMIT),
        name="prompt_attn",
    )(zb, zb, z32, zb, zb, zb)


def _scan_matrix():
    t = np.tril(np.ones((BLK, BLK), np.float32))
    mats = [t]
    for lvl in range(1, 8):
        bs = BLK >> (lvl - 1)
        r = np.arange(BLK)
        anchor = (r // bs) * bs + bs // 2 - 1
        mats.append(t[anchor])
    return np.concatenate(mats, axis=0)


def _level_signs():
    r = np.arange(BLK)
    rows = [np.where((r >> (7 - lvl)) & 1, 1.0, -1.0) for lvl in range(1, 8)]
    return np.repeat(np.concatenate(rows)[:, None], BLK, axis=1).astype(np.float32)


def _level_ids():
    t, s = np.meshgrid(np.arange(BLK), np.arange(BLK), indexing="ij")
    x = t ^ s
    hb = np.floor(np.log2(np.maximum(x, 1))).astype(np.int64)
    ids = np.where(s < t, 7 - hb, np.where(s == t, 8, 0))
    return ids.astype(np.float32)


def _prompt_hgrn_kernel(hq_ref, hf_ref, hi_ref, hg_ref, lbl_ref, ng_ref, scan_ref, sg_ref, lvl_ref,
                        o_ref, s_ref, st_s):
    c = pl.program_id(1)
    nc = pl.num_programs(1)

    @pl.when(c == 0)
    def _():
        st_s[...] = jnp.zeros_like(st_s)

    row = lax.broadcasted_iota(I32, (BLK, BLK), 0)
    valid = (c > 0) | (row >= N_PAD)
    ll0, ll1 = lbl_ref[0:1, :], lbl_ref[1:2, :]
    mx = jnp.maximum(ll0, ll1)
    e0, e1 = jnp.exp(ll0 - mx), jnp.exp(ll1 - mx)
    lb_all = e0 / (e0 + e1)
    ng = ng_ref[...]

    for h in range(N_HEADS):
        sl = slice(HEAD_DIM * h, HEAD_DIM * (h + 1))
        lb = lb_all[:, sl]
        f = lb + (1.0 - lb) * jax.nn.sigmoid(hf_ref[:, sl].astype(F32))
        lf = jnp.where(valid, jnp.log(f), 0.0)
        kf = jnp.where(valid, 1.0 - f, 0.0)
        hq = hq_ref[:, sl].astype(F32)
        qh = hq * jax.nn.sigmoid(hq)
        ih = hi_ref[:, sl]

        p0 = lf.astype(BF16)
        r1 = lf - p0.astype(F32)
        p1 = r1.astype(BF16)
        p2 = (r1 - p1.astype(F32)).astype(BF16)
        b3 = _nn(scan_ref[0:BLK, :], jnp.concatenate([p0, p1, p2], axis=1))
        b = b3[:, 0:BLK] + b3[:, BLK:2 * BLK] + b3[:, 2 * BLK:3 * BLK]
        anc = _nn(scan_ref[BLK:, :], p0)

        a = jnp.where(lvl_ref[...] == 8.0, _nt(qh.astype(BF16), kf.astype(BF16)), 0.0)
        for lvl in range(1, 8):
            rows = slice(BLK * (lvl - 1), BLK * lvl)
            dec = jnp.exp((b - anc[rows]) * sg_ref[rows, :])
            qk = _nt((qh * dec).astype(BF16), (kf * dec).astype(BF16))
            a = a + jnp.where(lvl_ref[...] == float(lvl), qk, 0.0)

        st = st_s[h]
        o = _nn(a.astype(BF16), ih) + _nt((qh * jnp.exp(b)).astype(BF16), st.astype(BF16))
        bend = b[BLK - 1:BLK, :]
        kend = (kf * jnp.exp(bend - b)).astype(BF16)
        st_new = st * jnp.exp(bend) + _nn(ih.astype(F32).T.astype(BF16), kend)
        st_s[h] = st_new

        on = o * lax.rsqrt(jnp.mean(o * o, axis=-1, keepdims=True) + NORM_EPS) * ng
        hg = hg_ref[:, sl].astype(F32)
        o_ref[:, sl] = (on * (hg * jax.nn.sigmoid(hg))).astype(BF16)

    @pl.when(c == nc - 1)
    def _():
        for h in range(N_HEADS):
            s_ref[0, h] = st_s[h].T


def _prompt_hgrn(zb, lb_logits, norm_g, nbatch, tp):
    nb = tp // BLK
    scan = jnp.asarray(_scan_matrix(), BF16)
    signs = jnp.asarray(_level_signs(), F32)
    level_ids = jnp.asarray(_level_ids(), F32)
    row_spec = lambda cb: pl.BlockSpec((BLK, 1024), lambda b, c: (b * nb + c, cb))
    return pl.pallas_call(
        _prompt_hgrn_kernel,
        grid=(nbatch, nb),
        in_specs=[
            row_spec(C_HQ // 1024), row_spec(C_HF // 1024), row_spec(C_HI // 1024), row_spec(C_HG // 1024),
            pl.BlockSpec((2, 1024), lambda b, c: (0, 0)),
            pl.BlockSpec((1, HEAD_DIM), lambda b, c: (0, 0)),
            pl.BlockSpec((8 * BLK, BLK), lambda b, c: (0, 0)),
            pl.BlockSpec((7 * BLK, BLK), lambda b, c: (0, 0)),
            pl.BlockSpec((BLK, BLK), lambda b, c: (0, 0)),
        ],
        out_specs=[
            pl.BlockSpec((BLK, 1024), lambda b, c: (b * nb + c, 0)),
            pl.BlockSpec((1, N_HEADS, HEAD_DIM, HEAD_DIM), lambda b, c: (b, 0, 0, 0)),
        ],
        out_shape=[
            jax.ShapeDtypeStruct((nbatch * tp, 1024), BF16),
            jax.ShapeDtypeStruct((nbatch, N_HEADS, HEAD_DIM, HEAD_DIM), F32),
        ],
        scratch_shapes=[pltpu.VMEM((N_HEADS, HEAD_DIM, HEAD_DIM), F32)],
        compiler_params=pltpu.CompilerParams(
            dimension_semantics=("arbitrary", "arbitrary"), vmem_limit_bytes=VMEM_LIMIT),
        name="prompt_hgrn",
    )(zb, zb, zb, zb, lb_logits, norm_g, scan, signs, level_ids)


PG = 16


def _dec_idx_proj_kernel(x_ref, g_ref, b_ref, w_ref, o_ref):
    xn = _layer_norm(x_ref[...], g_ref[...], b_ref[...])
    o_ref[...] = jnp.dot(xn, w_ref[...], preferred_element_type=F32, precision=lax.Precision.HIGHEST)


def _dec_idx_proj(x_s, g, b, w_idx):
    return pl.pallas_call(
        _dec_idx_proj_kernel,
        out_shape=jax.ShapeDtypeStruct((x_s.shape[0], w_idx.shape[1]), F32),
        compiler_params=pltpu.CompilerParams(vmem_limit_bytes=VMEM_LIMIT),
        name="dec_idx_proj",
    )(x_s, g, b, w_idx)


PGS = 32


def _dec_score_kernel(pt_ref, qi_ref, w_ref, *refs):
    kp_refs, out_ref = refs[:PGS], refs[PGS]
    qi = qi_ref[0]
    w = w_ref[0] * (IDX_SCALE * IDX_W_SCALE)
    for i in range(PGS):
        qk = jnp.dot(qi, kp_refs[i][0, 0], preferred_element_type=F32, precision=lax.Precision.HIGHEST)
        out_ref[0, i] = jnp.sum(jnp.maximum(qk, 0.0) * w, axis=0, keepdims=True)


def _dec_scores(page_table, qi_s, wcol, kidx_t):
    nbatch, npages = page_table.shape
    page_spec = lambda i: pl.BlockSpec(
        (1, 1, IDX_DIM, PAGE), lambda b, p, pt: (0, pt[b, p * PGS + i], 0, 0))
    return pl.pallas_call(
        _dec_score_kernel,
        grid_spec=pltpu.PrefetchScalarGridSpec(
            num_scalar_prefetch=1,
            grid=(nbatch, npages // PGS),
            in_specs=[
                pl.BlockSpec((1, N_HEADS, IDX_DIM), lambda b, p, pt: (b, 0, 0)),
                pl.BlockSpec((1, N_HEADS, 1), lambda b, p, pt: (b, 0, 0)),
            ] + [page_spec(i) for i in range(PGS)],
            out_specs=pl.BlockSpec((1, PGS, 1, PAGE), lambda b, p, pt: (b, p, 0, 0)),
        ),
        out_shape=jax.ShapeDtypeStruct((nbatch, npages, 1, PAGE), F32),
        compiler_params=pltpu.CompilerParams(dimension_semantics=("arbitrary", "arbitrary")),
        name="dec_scores",
    )(page_table, qi_s, wcol, *([kidx_t] * PGS))


def _dec_select_kernel(sc_ref, qi_ref, kin_ref, w_ref, nd_ref, ndn_ref, key_s):
    nq, npast = sc_ref.shape
    col = lax.broadcasted_iota(I32, (nq, npast), 1)
    key_s[...] = _sort_key(sc_ref[...])

    kin = kin_ref[...]
    w = w_ref[...] * (IDX_SCALE * IDX_W_SCALE)
    snew = jnp.zeros((nq, 1), F32)
    for h in range(N_HEADS):
        sh = jnp.sum(qi_ref[h] * kin, axis=-1, keepdims=True)
        snew = snew + jnp.maximum(sh, 0.0) * w[:, h:h + 1]
    keyn = _sort_key(snew)

    def count(pred_past, pred_new):
        c = jnp.sum(jnp.where(pred_past(key_s[...]), 1.0, 0.0), axis=-1, keepdims=True)
        return c + jnp.where(pred_new(keyn), 1.0, 0.0)

    def radix_body(i, p):
        cand = p + lax.shift_left(jnp.int32(1), 31 - i)
        return jnp.where(count(lambda k: k >= cand, lambda k: k >= cand) >= TOPK, cand, p)

    p = lax.fori_loop(0, 32, radix_body, jnp.full((nq, 1), INT_MIN, I32))
    n_gt = count(lambda k: k > p, lambda k: k > p)
    n_ge = count(lambda k: k >= p, lambda k: k >= p)
    need = TOPK - n_gt
    overflow = n_ge > TOPK

    def tie_body(i, x):
        cand = x + lax.shift_left(jnp.int32(1), 13 - i)
        g = count(lambda k: (k == p) & (col < cand), lambda k: (k == p) & (npast < cand))
        return jnp.where(g < need, cand, x)

    x = lax.fori_loop(0, 14, tie_body, jnp.zeros((nq, 1), I32))
    cut = jnp.where(overflow, x, INT_MAX)

    key = key_s[...]
    sel = (key > p) | ((key == p) & (col <= cut))
    nd_ref[...] = jnp.where(sel, -(npast - col).astype(F32), -jnp.inf)
    seln = (keyn > p) | ((keyn == p) & (npast <= cut))
    lane = lax.broadcasted_iota(I32, ndn_ref.shape, 1)
    ndn_ref[...] = jnp.where((lane == 0) & seln, 0.0, -jnp.inf)


def _dec_select(sc, qi_hm, ki_new, wrow):
    nq, npast = sc.shape
    return pl.pallas_call(
        _dec_select_kernel,
        out_shape=[jax.ShapeDtypeStruct((nq, npast), F32), jax.ShapeDtypeStruct((nq, PAGE), F32)],
        scratch_shapes=[pltpu.VMEM((nq, npast), I32)],
        compiler_params=pltpu.CompilerParams(vmem_limit_bytes=VMEM_LIMIT),
        name="dec_select",
    )(sc, qi_hm, ki_new, wrow)


def _dec_attn_kernel(pt_ref, q_ref, nd_ref, ndn_ref, kn_ref, vn_ref, *refs):
    k_refs, v_refs = refs[:PG], refs[PG:2 * PG]
    o_ref, m_s, l_s, acc_s = refs[2 * PG:]
    p = pl.program_id(1)
    hrow = lax.broadcasted_iota(I32, (N_HEADS, PAGE * N_KV), 0)
    ccol = lax.broadcasted_iota(I32, (N_HEADS, PAGE * N_KV), 1)
    own = (ccol % N_KV) == (hrow // GROUP)
    slope = jnp.exp2(-(hrow + 1).astype(F32))
    q = q_ref[0]

    def update(lgs, pv):
        m_old = m_s[:, 0:1]
        mx = lgs[0]
        for t in lgs[1:]:
            mx = jnp.maximum(mx, t)
        m_new = jnp.maximum(m_old, jnp.max(mx, axis=-1, keepdims=True))
        m_safe = jnp.where(m_new == -jnp.inf, 0.0, m_new)
        alpha = jnp.exp(m_old - m_safe)
        ps = [jnp.exp(t - m_safe) for t in lgs]
        ls = ps[0]
        for t in ps[1:]:
            ls = ls + t
        l_s[...] = alpha * l_s[...] + jnp.sum(ls, axis=-1, keepdims=True)
        acc_s[...] = alpha * acc_s[...] + pv(ps)
        m_s[...] = jnp.broadcast_to(m_new, m_s.shape)

    @pl.when(p == 0)
    def _():
        m_s[...] = jnp.full_like(m_s, -jnp.inf)
        l_s[...] = jnp.zeros_like(l_s)
        acc_s[...] = jnp.zeros_like(acc_s)
        ln = jnp.sum(q.astype(F32) * kn_ref[0].astype(F32), axis=-1, keepdims=True) * ATTN_SCALE
        lg = jnp.broadcast_to(ln, (N_HEADS, PAGE)) + ndn_ref[0]
        vn = vn_ref[0].astype(F32)
        update([lg], lambda ps: jnp.sum(ps[0], axis=-1, keepdims=True) * vn)

    lgs = []
    for i in range(PG):
        qk = _nt(q, k_refs[i][0].astype(BF16))
        lgs.append(jnp.where(own, qk * ATTN_SCALE + slope * nd_ref[0, i], -jnp.inf))

    def pv(ps):
        out = jnp.zeros((N_HEADS, HEAD_DIM), F32)
        for i in range(PG):
            out = out + _nn(ps[i].astype(BF16), v_refs[i][0].astype(BF16))
        return out

    update(lgs, pv)

    @pl.when(p == pl.num_programs(1) - 1)
    def _():
        l = l_s[...]
        o_ref[0] = (acc_s[...] / jnp.where(l == 0.0, 1.0, l)).astype(BF16)


def _dec_attn(page_table, q8, nd4, ndn, kn8, vn8, k_flat, v_flat):
    nbatch, npages = page_table.shape
    page_spec = lambda i: pl.BlockSpec(
        (1, PAGE * N_KV, HEAD_DIM), lambda b, p, pt: (pt[b, p * PG + i], 0, 0))
    head_spec = pl.BlockSpec((1, N_HEADS, HEAD_DIM), lambda b, p, pt: (b, 0, 0))
    return pl.pallas_call(
        _dec_attn_kernel,
        grid_spec=pltpu.PrefetchScalarGridSpec(
            num_scalar_prefetch=1,
            grid=(nbatch, npages // PG),
            in_specs=[
                head_spec,
                pl.BlockSpec((1, PG, 1, PAGE * N_KV), lambda b, p, pt: (b, p, 0, 0)),
                pl.BlockSpec((1, 1, PAGE), lambda b, p, pt: (b, 0, 0)),
                head_spec, head_spec,
            ] + [page_spec(i) for i in range(PG)] + [page_spec(i) for i in range(PG)],
            out_specs=head_spec,
            scratch_shapes=[pltpu.VMEM((N_HEADS, PAGE), F32)] * 3,
        ),
        out_shape=jax.ShapeDtypeStruct((nbatch, N_HEADS, HEAD_DIM), BF16),
        compiler_params=pltpu.CompilerParams(
            dimension_semantics=("arbitrary", "arbitrary"), vmem_limit_bytes=VMEM_LIMIT),
        name="dec_attn",
    )(page_table, q8, nd4, ndn, kn8, vn8, *([k_flat] * PG), *([v_flat] * PG))


def _dec_hgrn_kernel(s_ref, hq_ref, hf_ref, hi_ref, hg_ref, lbl_ref, ng_ref, so_ref, o_ref):
    ll0, ll1 = lbl_ref[0], lbl_ref[1]
    mx = jnp.maximum(ll0, ll1)
    e0, e1 = jnp.exp(ll0 - mx), jnp.exp(ll1 - mx)
    lb = e0 / (e0 + e1)
    f = lb + (1.0 - lb) * jax.nn.sigmoid(hf_ref[0])
    hq = hq_ref[0]
    q = hq * jax.nn.sigmoid(hq)
    ng = ng_ref[...]
    for h in range(N_HEADS):
        s_new = s_ref[0, 0, h] * f[h] + (1.0 - f[h]) * hi_ref[0, h]
        so_ref[0, h] = s_new
        o = jnp.sum(s_new * q[h], axis=0, keepdims=True)
        on = o * lax.rsqrt(jnp.mean(o * o, axis=-1, keepdims=True) + NORM_EPS) * ng
        hg = hg_ref[0, h]
        o_ref[0, h] = (on * (hg * jax.nn.sigmoid(hg))).astype(BF16)


def _dec_hgrn(state, hq_c, hf_c, hi_r, hg_r, lbl_c, norm_g):
    nbatch = state.shape[1]
    col_spec = pl.BlockSpec((1, N_HEADS, HEAD_DIM, 1), lambda b: (b, 0, 0, 0))
    row_spec = pl.BlockSpec((1, N_HEADS, 1, HEAD_DIM), lambda b: (b, 0, 0, 0))
    return pl.pallas_call(
        _dec_hgrn_kernel,
        grid=(nbatch,),
        in_specs=[
            pl.BlockSpec((1, 1, N_HEADS, HEAD_DIM, HEAD_DIM), lambda b: (0, b, 0, 0, 0)),
            col_spec, col_spec, row_spec, row_spec,
            pl.BlockSpec((2, N_HEADS, HEAD_DIM, 1), lambda b: (0, 0, 0, 0)),
            pl.BlockSpec((1, HEAD_DIM), lambda b: (0, 0)),
        ],
        out_specs=[
            pl.BlockSpec((1, N_HEADS, HEAD_DIM, HEAD_DIM), lambda b: (b, 0, 0, 0)),
            row_spec,
        ],
        out_shape=[
            jax.ShapeDtypeStruct((nbatch, N_HEADS, HEAD_DIM, HEAD_DIM), F32),
            jax.ShapeDtypeStruct((nbatch, N_HEADS, 1, HEAD_DIM), BF16),
        ],
        compiler_params=pltpu.CompilerParams(dimension_semantics=("arbitrary",)),
        name="dec_hgrn",
    )(state, hq_c, hf_c, hi_r, hg_r, lbl_c, norm_g)


def _outproj_kernel(x_ref, gi_ref, bi_ref, a_ref, h_ref, wa_ref, wh_ref, g_ref, b_ref, o_ref):
    xn = _layer_norm(x_ref[...], gi_ref[...], bi_ref[...])
    mix = _nn(a_ref[...], wa_ref[...]) + _nn(h_ref[...], wh_ref[...])
    o_ref[...] = _layer_norm(ALPHA * xn + mix, g_ref[...], b_ref[...])


def _outproj(x_all, gi, bi, attn, hgm, wa, wh, g, b, tm):
    r = x_all.shape[0]
    vec = pl.BlockSpec((1, D_MODEL), lambda i: (0, 0))
    return pl.pallas_call(
        _outproj_kernel,
        grid=(r // tm,),
        in_specs=[
            pl.BlockSpec((tm, D_MODEL), lambda i: (i, 0)), vec, vec,
            pl.BlockSpec((tm, 1024), lambda i: (i, 0)),
            pl.BlockSpec((tm, 1024), lambda i: (i, 0)),
            pl.BlockSpec((1024, D_MODEL), lambda i: (0, 0)),
            pl.BlockSpec((1024, D_MODEL), lambda i: (0, 0)),
            vec, vec,
        ],
        out_specs=pl.BlockSpec((tm, D_MODEL), lambda i: (i, 0)),
        out_shape=jax.ShapeDtypeStruct((r, D_MODEL), F32),
        compiler_params=pltpu.CompilerParams(
            dimension_semantics=("arbitrary",), vmem_limit_bytes=VMEM_LIMIT),
        name="outproj",
    )(x_all, gi, bi, attn, hgm, wa, wh, g, b)


def _ffn_kernel(x_ref, wg_ref, wu_ref, wd_ref, g_ref, b_ref, y_ref, xb_ref):
    f = pl.program_id(1)

    @pl.when(f == 0)
    def _():
        xb_ref[...] = x_ref[...].astype(BF16)
        y_ref[...] = jnp.zeros_like(y_ref)

    xb = xb_ref[...]
    gate = _nn(xb, wg_ref[...])
    up = _nn(xb, wu_ref[...])
    hid = (gate * jax.nn.sigmoid(gate) * up).astype(BF16)
    y_ref[...] += _nn(hid, wd_ref[...])

    @pl.when(f == pl.num_programs(1) - 1)
    def _():
        y_ref[...] = _layer_norm(ALPHA * x_ref[...] + y_ref[...], g_ref[...], b_ref[...])


def _ffn(x1, wg, wu, wd, g, b, tm, tf):
    r = x1.shape[0]
    dff = wg.shape[1]
    vec = pl.BlockSpec((1, D_MODEL), lambda i, f: (0, 0))
    return pl.pallas_call(
        _ffn_kernel,
        grid=(r // tm, dff // tf),
        in_specs=[
            pl.BlockSpec((tm, D_MODEL), lambda i, f: (i, 0)),
            pl.BlockSpec((D_MODEL, tf), lambda i, f: (0, f)),
            pl.BlockSpec((D_MODEL, tf), lambda i, f: (0, f)),
            pl.BlockSpec((tf, D_MODEL), lambda i, f: (f, 0)),
            vec, vec,
        ],
        out_specs=pl.BlockSpec((tm, D_MODEL), lambda i, f: (i, 0)),
        out_shape=jax.ShapeDtypeStruct((r, D_MODEL), F32),
        scratch_shapes=[pltpu.VMEM((tm, D_MODEL), BF16)],
        compiler_params=pltpu.CompilerParams(
            dimension_semantics=("arbitrary", "arbitrary"), vmem_limit_bytes=VMEM_LIMIT),
        name="ffn",
    )(x1, wg, wu, wd, g, b)


def _reorder_w_in(w):
    q, k, v, qi, wi, ki, hq, hf, hi, hg = jnp.split(
        w, [1024, 1536, 2048, 2560, 2568, 2632, 3656, 4680, 5704], axis=1)
    pad = jnp.zeros((w.shape[0], 128 - wi.shape[1]), w.dtype)
    return jnp.concatenate([q, hq, hf, hi, hg, k, v, qi, ki, ki, wi, pad], axis=1).astype(BF16)


def kernel(x_prompt, x_sample, cache_k, cache_v, cache_kidx, state_hgrn, page_table, meta_tokens,
           ln_in_g, ln_in_b, w_in, hg_lb_logits, hg_norm_g, w_out, ln1_g, ln1_b, w_gate, w_up,
           w_down, ln2_g, ln2_b):
    nbatch, seq, _ = x_prompt.shape
    ndec = x_sample.shape[0]
    assert w_in.shape[0] == 1 and x_sample.shape[1] == 1 and seq % BLK == 0
    tp = BLK + seq
    rp = nbatch * tp
    ntail = DEC_ROWS - ndec
    r = rp + DEC_ROWS
    assert ntail >= 0 and r % (20 * 16) == 0

    head = jnp.concatenate([jnp.zeros((N_PAD, D_MODEL), F32), meta_tokens.astype(F32)], axis=0)
    parts = []
    for b in range(nbatch):
        parts += [head, x_prompt[b]]
    x_all = jnp.concatenate(
        parts + [x_sample.reshape(ndec, D_MODEL), jnp.zeros((ntail, D_MODEL), F32)], axis=0)

    row = lambda a: a.reshape(1, -1).astype(F32)
    tm = r // 10
    z32, zb = _inproj(x_all, row(ln_in_g), row(ln_in_b), _reorder_w_in(w_in[0]), tm, 1152)

    lbl = hg_lb_logits.astype(F32)
    ng = row(hg_norm_g[0])
    attn_p = _prompt_attn(z32, zb, nbatch, tp)
    hgm_p, s_p = _prompt_hgrn(zb, lbl, ng, nbatch, tp)

    zs32, zbs = z32[rp:rp + ndec], zb[rp:rp + ndec]
    npast = page_table.shape[1] * PAGE
    wi0 = w_in[0]
    w_idx = jnp.concatenate([wi0[:, 2048:2560], wi0[:, 2568:2632], wi0[:, 2560:2568],
                             jnp.zeros((D_MODEL, 56), F32)], axis=1)
    z_idx = _dec_idx_proj(x_sample.reshape(ndec, D_MODEL), row(ln_in_g), row(ln_in_b), w_idx)
    qi_s = z_idx[:, 0:512].reshape(ndec, N_HEADS, IDX_DIM)
    wrow = z_idx[:, 576:576 + N_HEADS]
    kidx_t = jnp.swapaxes(cache_kidx, 2, 3)
    k_flat = cache_k[0].reshape(-1, PAGE * N_KV, HEAD_DIM)
    v_flat = cache_v[0].reshape(-1, PAGE * N_KV, HEAD_DIM)
    sc = _dec_scores(page_table, qi_s, wrow[:, :, None], kidx_t).reshape(ndec, npast)
    nd, ndn = _dec_select(sc, jnp.swapaxes(qi_s, 0, 1), z_idx[:, 512:576], wrow)
    nd4 = jnp.repeat(nd, N_KV, axis=1).reshape(ndec, npast // PAGE, 1, PAGE * N_KV)
    rep = lambda a: jnp.repeat(a.reshape(ndec, N_KV, HEAD_DIM), N_HEADS // N_KV, axis=1)
    attn_s = _dec_attn(page_table, zbs[:, C_Q:C_Q + 1024].reshape(ndec, N_HEADS, HEAD_DIM),
                       nd4, ndn.reshape(ndec, 1, PAGE),
                       rep(zbs[:, C_K:C_K + 512]), rep(zbs[:, C_V:C_V + 512]), k_flat, v_flat)
    colv = lambda c0: zbs[:, c0:c0 + 1024].astype(F32).reshape(ndec, N_HEADS, HEAD_DIM, 1)
    rowv = lambda c0: zbs[:, c0:c0 + 1024].astype(F32).reshape(ndec, N_HEADS, 1, HEAD_DIM)
    s_s, hgm_s = _dec_hgrn(state_hgrn, colv(C_HQ), colv(C_HF), rowv(C_HI), rowv(C_HG),
                           lbl.reshape(2, N_HEADS, HEAD_DIM, 1), ng)

    tail = jnp.zeros((ntail, 1024), BF16)
    attn = jnp.concatenate([attn_p, attn_s.reshape(ndec, 1024), tail], axis=0)
    hgm = jnp.concatenate([hgm_p, hgm_s.reshape(ndec, 1024), tail], axis=0)
    wo = w_out[0].astype(BF16)
    x1 = _outproj(x_all, row(ln_in_g), row(ln_in_b), attn, hgm, wo[:1024], wo[1024:],
                  row(ln1_g[0]), row(ln1_b[0]), r // 20)
    y = _ffn(x1, w_gate[0].astype(BF16), w_up[0].astype(BF16), w_down[0].astype(BF16),
             row(ln2_g[0]), row(ln2_b[0]), tm, 512)

    def prompt_cols(a, c0, width, first_row):
        return jnp.stack([lax.slice(a, (b * tp + first_row, c0), ((b + 1) * tp, c0 + width))
                          for b in range(nbatch)])

    y_prompt = prompt_cols(y, 0, D_MODEL, BLK)
    y_sample = y[rp:rp + ndec].reshape(ndec, 1, D_MODEL)
    ck, cv, ckk = C_K - C32, C_V - C32, C_KK - C32
    k_p = prompt_cols(z32, ck, 512, N_PAD).reshape(1, nbatch, tp - N_PAD, N_KV, HEAD_DIM)
    v_p = prompt_cols(z32, cv, 512, N_PAD).reshape(1, nbatch, tp - N_PAD, N_KV, HEAD_DIM)
    ki_p = prompt_cols(z32, ckk, IDX_DIM, N_PAD)[None]
    k_s = zs32[:, ck:ck + 512].reshape(1, ndec, 1, N_KV, HEAD_DIM)
    v_s = zs32[:, cv:cv + 512].reshape(1, ndec, 1, N_KV, HEAD_DIM)
    ki_s = zs32[:, ckk:ckk + IDX_DIM].reshape(1, ndec, 1, IDX_DIM)
    return (y_prompt, y_sample, k_p, v_p, ki_p, s_p[None], k_s, v_s, ki_s, s_s[None])
```

```python
import functools

import numpy as np
import jax
import jax.numpy as jnp
from jax import lax
from jax.experimental import pallas as pl
from jax.experimental.pallas import tpu as pltpu

F32 = jnp.float32
BF16 = jnp.bfloat16
I32 = jnp.int32

D_MODEL = 2048
N_META = 16
BLK = 128
N_PAD = BLK - N_META
N_HEADS = 8
N_KV = 4
HEAD_DIM = 128
IDX_DIM = 64
TOPK = 256
PAGE = 128
LN_EPS = 1e-5
NORM_EPS = 1e-6
ALPHA = 2.0 ** 0.25
ATTN_SCALE = HEAD_DIM ** -0.5
IDX_SCALE = IDX_DIM ** -0.5
IDX_W_SCALE = N_HEADS ** -0.5
INT_MIN = -2 ** 31
INT_MAX = 2 ** 31 - 1
DEC_ROWS = 192

C_Q, C_HQ, C_HF, C_HI, C_HG = 0, 1024, 2048, 3072, 4096
C_K, C_V, C_QI, C_KK, C_WI = 5120, 5632, 6144, 6656, 6784
ZW = 6912
C32 = 4608

VMEM_LIMIT = 56 * 1024 * 1024


def _nt(a, b):
    return lax.dot_general(a, b, (((1,), (1,)), ((), ())), preferred_element_type=F32)


def _nn(a, b):
    return jnp.dot(a, b, preferred_element_type=F32)


def _layer_norm(x, g, b):
    mu = jnp.mean(x, axis=-1, keepdims=True)
    xc = x - mu
    var = jnp.mean(xc * xc, axis=-1, keepdims=True)
    return xc * lax.rsqrt(var + LN_EPS) * g + b


KEY_NEG_INF = INT_MIN + 0x7FFFFF


def _key_to_float(k):
    return pltpu.bitcast(k ^ ((k >> 31) & INT_MAX), F32)


def _sort_key(s):
    s = jnp.where(s == 0.0, 0.0, s)
    bits = pltpu.bitcast(s, I32)
    return bits ^ ((bits >> 31) & INT_MAX)


def _inproj_kernel(x_ref, g_ref, b_ref, w_ref, z_ref, zb_ref, xs_ref, *, first_f32_tile):
    n = pl.program_id(1)

    @pl.when(n == 0)
    def _():
        xs_ref[...] = _layer_norm(x_ref[...], g_ref[...], b_ref[...]).astype(BF16)

    acc = _nt(xs_ref[...], w_ref[...])
    zb_ref[...] = acc.astype(BF16)

    @pl.when(n >= first_f32_tile)
    def _():
        z_ref[...] = acc


def _inproj(x_all, g, b, w, tm, tn):
    r = x_all.shape[0]
    assert C32 % tn == 0 and ZW % tn == 0
    n0 = C32 // tn
    return pl.pallas_call(
        functools.partial(_inproj_kernel, first_f32_tile=n0),
        grid=(r // tm, ZW // tn),
        in_specs=[
            pl.BlockSpec((tm, D_MODEL), lambda i, n: (i, 0)),
            pl.BlockSpec((1, D_MODEL), lambda i, n: (0, 0)),
            pl.BlockSpec((1, D_MODEL), lambda i, n: (0, 0)),
            pl.BlockSpec((tn, D_MODEL), lambda i, n: (n, 0)),
        ],
        out_specs=[
            pl.BlockSpec((tm, tn), lambda i, n: (i, jnp.maximum(n - n0, 0))),
            pl.BlockSpec((tm, tn), lambda i, n: (i, n)),
        ],
        out_shape=[jax.ShapeDtypeStruct((r, ZW - C32), F32), jax.ShapeDtypeStruct((r, ZW), BF16)],
        scratch_shapes=[pltpu.VMEM((tm, D_MODEL), BF16)],
        compiler_params=pltpu.CompilerParams(
            dimension_semantics=("arbitrary", "arbitrary"), vmem_limit_bytes=VMEM_LIMIT),
        name="inproj",
    )(x_all, g, b, w)


KCH = 3
KC = KCH * BLK
GROUP = N_HEADS // N_KV


def _fold(a):
    out = a[:, 0:BLK]
    for i in range(1, KCH):
        out = out + a[:, BLK * i:BLK * (i + 1)]
    return out


def _tile3(a):
    return jnp.concatenate([a] * KCH, axis=1)


def _prompt_attn_kernel(q_ref, qi_ref, wi_ref, k_ref, v_ref, kk_ref, o_ref,
                        sc_s, nd_s, lg_s, wb_s, qm_s, cut_s, acc_s, qs_s, m_s):
    j = pl.program_id(1)
    nch = (j + KCH) // KCH
    row = lax.broadcasted_iota(I32, (BLK, KC), 0)
    col = lax.broadcasted_iota(I32, (BLK, KC), 1)
    qrow = j * BLK + row
    zeros = jnp.zeros((BLK, BLK), F32)

    wi = wi_ref[...] * (IDX_SCALE * IDX_W_SCALE)
    qi = qi_ref[...]
    lane = lax.broadcasted_iota(I32, (BLK, BLK), 1)
    for h in range(N_HEADS):
        wb_s[h] = jnp.broadcast_to(wi[:, h:h + 1], (BLK, BLK))
        q2 = qi[:, BLK * (h // 2):BLK * (h // 2 + 1)]
        keep = (lane < IDX_DIM) if h % 2 == 0 else (lane >= IDX_DIM)
        qm_s[BLK * h:BLK * (h + 1), :] = jnp.where(keep, q2, jnp.zeros_like(q2))

    def score_body(c, carry):
        k0 = pl.multiple_of(c * KC, BLK)
        sh = _nt(qm_s[...], kk_ref[pl.ds(k0, KC), :])
        s = jnp.zeros((BLK, KC), F32)
        for h in range(N_HEADS):
            s = s + jnp.maximum(sh[BLK * h:BLK * (h + 1)], 0.0) * _tile3(wb_s[h])
        krow = k0 + col
        adm = (krow >= N_PAD) & (krow <= qrow)
        sc_s[c] = jnp.where(adm, s, -jnp.inf)
        return carry

    lax.fori_loop(0, nch, score_body, 0)

    def count(pred):
        body = lambda c, a: a + _fold(jnp.where(pred(c, sc_s[c]), 1.0, 0.0))
        return jnp.sum(lax.fori_loop(0, nch, body, zeros), axis=-1, keepdims=True)

    def radix_body(i, p):
        cand = p + lax.shift_left(jnp.int32(1), 31 - i)
        candb = jnp.broadcast_to(_key_to_float(cand), (BLK, KC))
        return jnp.where(count(lambda c, s: s >= candb) >= TOPK, cand, p)

    p0 = jnp.full((BLK, 1), INT_MIN, I32)
    p = lax.cond(j * BLK + N_META > TOPK, lambda: lax.fori_loop(0, 32, radix_body, p0), lambda: p0)
    p = jnp.maximum(p, KEY_NEG_INF)
    pf = _key_to_float(p)
    pb = jnp.broadcast_to(pf, (BLK, KC))
    n_gt = count(lambda c, s: s > pb)
    n_ge = count(lambda c, s: s >= pb)
    need = TOPK - n_gt
    overflow = (n_ge > TOPK) & (pf > -jnp.inf)

    cut_s[...] = jnp.full((BLK, KC), INT_MAX, I32)

    @pl.when(jnp.max(jnp.where(overflow, 1.0, 0.0)) > 0.0)
    def _():
        def tie_body(i, x):
            cand = x + lax.shift_left(jnp.int32(1), 12 - i)
            candb = jnp.broadcast_to(cand, (BLK, KC))
            g = count(lambda c, s: (s == pb) & (c * KC + col < candb))
            return jnp.where(g < need, cand, x)

        x = lax.fori_loop(0, 13, tie_body, jnp.zeros((BLK, 1), I32))
        cut_s[...] = jnp.broadcast_to(jnp.where(overflow, x, INT_MAX), (BLK, KC))

    cutb = cut_s[...]

    def nd_body(c, carry):
        s = sc_s[c]
        krow = c * KC + col
        sel = ((s > pb) | ((s == pb) & (krow <= cutb))) & (s > -jnp.inf)
        nd_s[c] = jnp.where(sel, -(qrow - krow).astype(F32), -jnp.inf)
        return carry

    lax.fori_loop(0, nch, nd_body, 0)

    for h in range(N_HEADS):
        qs_s[BLK * h:BLK * (h + 1), :] = q_ref[:, HEAD_DIM * h:HEAD_DIM * (h + 1)]
    m_s[...] = jnp.full_like(m_s, -jnp.inf)
    acc_s[...] = jnp.zeros_like(acc_s)
    gr = GROUP * BLK

    def logit_body(c, carry):
        k0 = pl.multiple_of(c * KC, BLK)
        nd = nd_s[c]
        for g in range(N_KV):
            qk = _nt(qs_s[gr * g:gr * (g + 1), :], k_ref[pl.ds(k0, KC), HEAD_DIM * g:HEAD_DIM * (g + 1)])
            for i in range(GROUP):
                h = GROUP * g + i
                lg = qk[BLK * i:BLK * (i + 1)] * ATTN_SCALE + 2.0 ** -(h + 1) * nd
                lg_s[c, BLK * h:BLK * (h + 1), :] = lg
                m = m_s[BLK * h:BLK * (h + 1), :]
                for t in range(KCH):
                    m = jnp.maximum(m, lg[:, BLK * t:BLK * (t + 1)])
                m_s[BLK * h:BLK * (h + 1), :] = m
        return carry

    lax.fori_loop(0, nch, logit_body, 0)
    m = jnp.max(m_s[...], axis=-1, keepdims=True)
    m_s[...] = jnp.broadcast_to(jnp.where(m == -jnp.inf, 0.0, m), m_s.shape)

    ones = jnp.ones((KC, HEAD_DIM), BF16)

    def pv_body(c, carry):
        k0 = pl.multiple_of(c * KC, BLK)
        for g in range(N_KV):
            rows = slice(gr * g, gr * (g + 1))
            pr = jnp.exp(lg_s[c, rows, :] - _tile3(m_s[rows, :])).astype(BF16)
            v1 = jnp.concatenate([v_ref[pl.ds(k0, KC), HEAD_DIM * g:HEAD_DIM * (g + 1)], ones], axis=1)
            acc_s[rows, :] += _nn(pr, v1)
        return carry

    lax.fori_loop(0, nch, pv_body, 0)
    for h in range(N_HEADS):
        acc = acc_s[BLK * h:BLK * (h + 1), :]
        l = acc[:, HEAD_DIM:2 * HEAD_DIM]
        o_ref[:, HEAD_DIM * h:HEAD_DIM * (h + 1)] = (
            acc[:, 0:HEAD_DIM] / jnp.where(l == 0.0, 1.0, l)).astype(BF16)


def _prompt_attn(z32, zb, nbatch, tp):
    nb = tp // BLK
    assert nb % KCH == 0
    nc = nb // KCH
    return pl.pallas_call(
        _prompt_attn_kernel,
        grid=(nbatch, nb),
        in_specs=[
            pl.BlockSpec((BLK, 1024), lambda b, j: (b * nb + j, C_Q // 1024)),
            pl.BlockSpec((BLK, 512), lambda b, j: (b * nb + j, C_QI // 512)),
            pl.BlockSpec((BLK, 128), lambda b, j: (b * nb + j, (C_WI - C32) // 128)),
            pl.BlockSpec((tp, 512), lambda b, j: (b, C_K // 512)),
            pl.BlockSpec((tp, 512), lambda b, j: (b, C_V // 512)),
            pl.BlockSpec((tp, 128), lambda b, j: (b, C_KK // 128)),
        ],
        out_specs=pl.BlockSpec((BLK, 1024), lambda b, j: (b * nb + j, 0)),
        out_shape=jax.ShapeDtypeStruct((nbatch * tp, 1024), BF16),
        scratch_shapes=[
            pltpu.VMEM((nc, BLK, KC), F32),
            pltpu.VMEM((nc, BLK, KC), F32),
            pltpu.VMEM((nc, N_HEADS * BLK, KC), F32),
            pltpu.VMEM((N_HEADS, BLK, BLK), F32),
            pltpu.VMEM((N_HEADS * BLK, BLK), BF16),
            pltpu.VMEM((BLK, KC), I32),
            pltpu.VMEM((N_HEADS * BLK, 2 * HEAD_DIM), F32),
            pltpu.VMEM((N_HEADS * BLK, HEAD_DIM), BF16),
            pltpu.VMEM((N_HEADS * BLK, BLK), F32),
        ],
        compiler_params=pltpu.CompilerParams(
            dimension_semantics=("arbitrary", "arbitrary"), vmem_limit_bytes=VMEM_LIMIT),
        name="prompt_attn",
    )(zb, zb, z32, zb, zb, zb)


def _scan_matrix():
    t = np.tril(np.ones((BLK, BLK), np.float32))
    mats = [t]
    for lvl in range(1, 8):
        bs = BLK >> (lvl - 1)
        r = np.arange(BLK)
        anchor = (r // bs) * bs + bs // 2 - 1
        mats.append(t[anchor])
    return np.concatenate(mats, axis=0)


def _level_signs():
    r = np.arange(BLK)
    rows = [np.where((r >> (7 - lvl)) & 1, 1.0, -1.0) for lvl in range(1, 8)]
    return np.repeat(np.concatenate(rows)[:, None], BLK, axis=1).astype(np.float32)


def _level_ids():
    t, s = np.meshgrid(np.arange(BLK), np.arange(BLK), indexing="ij")
    x = t ^ s
    hb = np.floor(np.log2(np.maximum(x, 1))).astype(np.int64)
    ids = np.where(s < t, 7 - hb, np.where(s == t, 8, 0))
    return ids.astype(np.float32)


def _prompt_hgrn_kernel(hq_ref, hf_ref, hi_ref, hg_ref, lbl_ref, ng_ref, scan_ref, sg_ref, lvl_ref,
                        o_ref, s_ref, st_s):
    c = pl.program_id(1)
    nc = pl.num_programs(1)

    @pl.when(c == 0)
    def _():
        st_s[...] = jnp.zeros_like(st_s)

    row = lax.broadcasted_iota(I32, (BLK, BLK), 0)
    valid = (c > 0) | (row >= N_PAD)
    ll0, ll1 = lbl_ref[0:1, :], lbl_ref[1:2, :]
    mx = jnp.maximum(ll0, ll1)
    e0, e1 = jnp.exp(ll0 - mx), jnp.exp(ll1 - mx)
    lb_all = e0 / (e0 + e1)
    ng = ng_ref[...]

    for h in range(N_HEADS):
        sl = slice(HEAD_DIM * h, HEAD_DIM * (h + 1))
        lb = lb_all[:, sl]
        f = lb + (1.0 - lb) * jax.nn.sigmoid(hf_ref[:, sl].astype(F32))
        lf = jnp.where(valid, jnp.log(f), 0.0)
        kf = jnp.where(valid, 1.0 - f, 0.0)
        hq = hq_ref[:, sl].astype(F32)
        qh = hq * jax.nn.sigmoid(hq)
        ih = hi_ref[:, sl]

        p0 = lf.astype(BF16)
        r1 = lf - p0.astype(F32)
        p1 = r1.astype(BF16)
        p2 = (r1 - p1.astype(F32)).astype(BF16)
        b3 = _nn(scan_ref[0:BLK, :], jnp.concatenate([p0, p1, p2], axis=1))
        b = b3[:, 0:BLK] + b3[:, BLK:2 * BLK] + b3[:, 2 * BLK:3 * BLK]
        anc = _nn(scan_ref[BLK:, :], p0)

        a = jnp.where(lvl_ref[...] == 8.0, _nt(qh.astype(BF16), kf.astype(BF16)), 0.0)
        for lvl in range(1, 8):
            rows = slice(BLK * (lvl - 1), BLK * lvl)
            dec = jnp.exp((b - anc[rows]) * sg_ref[rows, :])
            qk = _nt((qh * dec).astype(BF16), (kf * dec).astype(BF16))
            a = a + jnp.where(lvl_ref[...] == float(lvl), qk, 0.0)

        st = st_s[h]
        o = _nn(a.astype(BF16), ih) + _nt((qh * jnp.exp(b)).astype(BF16), st.astype(BF16))
        bend = b[BLK - 1:BLK, :]
        kend = (kf * jnp.exp(bend - b)).astype(BF16)
        st_new = st * jnp.exp(bend) + _nn(ih.astype(F32).T.astype(BF16), kend)
        st_s[h] = st_new

        on = o * lax.rsqrt(jnp.mean(o * o, axis=-1, keepdims=True) + NORM_EPS) * ng
        hg = hg_ref[:, sl].astype(F32)
        o_ref[:, sl] = (on * (hg * jax.nn.sigmoid(hg))).astype(BF16)

    @pl.when(c == nc - 1)
    def _():
        for h in range(N_HEADS):
            s_ref[0, h] = st_s[h].T


def _prompt_hgrn(zb, lb_logits, norm_g, nbatch, tp):
    nb = tp // BLK
    scan = jnp.asarray(_scan_matrix(), BF16)
    signs = jnp.asarray(_level_signs(), F32)
    level_ids = jnp.asarray(_level_ids(), F32)
    row_spec = lambda cb: pl.BlockSpec((BLK, 1024), lambda b, c: (b * nb + c, cb))
    return pl.pallas_call(
        _prompt_hgrn_kernel,
        grid=(nbatch, nb),
        in_specs=[
            row_spec(C_HQ // 1024), row_spec(C_HF // 1024), row_spec(C_HI // 1024), row_spec(C_HG // 1024),
            pl.BlockSpec((2, 1024), lambda b, c: (0, 0)),
            pl.BlockSpec((1, HEAD_DIM), lambda b, c: (0, 0)),
            pl.BlockSpec((8 * BLK, BLK), lambda b, c: (0, 0)),
            pl.BlockSpec((7 * BLK, BLK), lambda b, c: (0, 0)),
            pl.BlockSpec((BLK, BLK), lambda b, c: (0, 0)),
        ],
        out_specs=[
            pl.BlockSpec((BLK, 1024), lambda b, c: (b * nb + c, 0)),
            pl.BlockSpec((1, N_HEADS, HEAD_DIM, HEAD_DIM), lambda b, c: (b, 0, 0, 0)),
        ],
        out_shape=[
            jax.ShapeDtypeStruct((nbatch * tp, 1024), BF16),
            jax.ShapeDtypeStruct((nbatch, N_HEADS, HEAD_DIM, HEAD_DIM), F32),
        ],
        scratch_shapes=[pltpu.VMEM((N_HEADS, HEAD_DIM, HEAD_DIM), F32)],
        compiler_params=pltpu.CompilerParams(
            dimension_semantics=("arbitrary", "arbitrary"), vmem_limit_bytes=VMEM_LIMIT),
        name="prompt_hgrn",
    )(zb, zb, zb, zb, lb_logits, norm_g, scan, signs, level_ids)


PG = 16


def _dec_idx_proj_kernel(x_ref, g_ref, b_ref, w_ref, o_ref):
    xn = _layer_norm(x_ref[...], g_ref[...], b_ref[...])
    o_ref[...] = lax.dot_general(xn, w_ref[...], (((1,), (1,)), ((), ())),
                                 preferred_element_type=F32, precision=lax.Precision.HIGHEST)


def _dec_idx_proj(x_s, g, b, w_idx):
    return pl.pallas_call(
        _dec_idx_proj_kernel,
        out_shape=jax.ShapeDtypeStruct((x_s.shape[0], w_idx.shape[0]), F32),
        compiler_params=pltpu.CompilerParams(vmem_limit_bytes=VMEM_LIMIT),
        name="dec_idx_proj",
    )(x_s, g, b, w_idx)


PGS = 64


def _dec_score_kernel(pt_ref, qi_ref, w_ref, *refs):
    kp_refs, out_ref = refs[:PGS], refs[PGS]
    qi = qi_ref[0]
    w = w_ref[0] * (IDX_SCALE * IDX_W_SCALE)
    for i in range(PGS):
        qk = jnp.dot(qi, kp_refs[i][0, 0], preferred_element_type=F32, precision=lax.Precision.HIGHEST)
        out_ref[0, i] = jnp.sum(jnp.maximum(qk, 0.0) * w, axis=0, keepdims=True)


def _dec_scores(page_table, qi_s, wcol, kidx_t):
    nbatch, npages = page_table.shape
    page_spec = lambda i: pl.BlockSpec(
        (1, 1, IDX_DIM, PAGE), lambda b, p, pt: (0, pt[b, p * PGS + i], 0, 0))
    return pl.pallas_call(
        _dec_score_kernel,
        grid_spec=pltpu.PrefetchScalarGridSpec(
            num_scalar_prefetch=1,
            grid=(nbatch, npages // PGS),
            in_specs=[
                pl.BlockSpec((1, N_HEADS, IDX_DIM), lambda b, p, pt: (b, 0, 0)),
                pl.BlockSpec((1, N_HEADS, 1), lambda b, p, pt: (b, 0, 0)),
            ] + [page_spec(i) for i in range(PGS)],
            out_specs=pl.BlockSpec((1, PGS, 1, PAGE), lambda b, p, pt: (b, p, 0, 0)),
        ),
        out_shape=jax.ShapeDtypeStruct((nbatch, npages, 1, PAGE), F32),
        compiler_params=pltpu.CompilerParams(dimension_semantics=("arbitrary", "arbitrary")),
        name="dec_scores",
    )(page_table, qi_s, wcol, *([kidx_t] * PGS))


def _dec_select_kernel(sc_ref, qi_ref, kin_ref, w_ref, nd_ref, ndn_ref, key_s):
    nq, npast = sc_ref.shape
    col = lax.broadcasted_iota(I32, (nq, npast), 1)
    key_s[...] = _sort_key(sc_ref[...])

    kin = kin_ref[...]
    w = w_ref[...] * (IDX_SCALE * IDX_W_SCALE)
    snew = jnp.zeros((nq, 1), F32)
    for h in range(N_HEADS):
        sh = jnp.sum(qi_ref[h] * kin, axis=-1, keepdims=True)
        snew = snew + jnp.maximum(sh, 0.0) * w[:, h:h + 1]
    keyn = _sort_key(snew)

    def count(pred_past, pred_new):
        c = jnp.sum(jnp.where(pred_past(key_s[...]), 1.0, 0.0), axis=-1, keepdims=True)
        return c + jnp.where(pred_new(keyn), 1.0, 0.0)

    def radix_body(i, p):
        cand = p + lax.shift_left(jnp.int32(1), 31 - i)
        return jnp.where(count(lambda k: k >= cand, lambda k: k >= cand) >= TOPK, cand, p)

    p = lax.fori_loop(0, 32, radix_body, jnp.full((nq, 1), INT_MIN, I32))
    n_gt = count(lambda k: k > p, lambda k: k > p)
    n_ge = count(lambda k: k >= p, lambda k: k >= p)
    need = TOPK - n_gt
    overflow = n_ge > TOPK

    def tie_body(i, x):
        cand = x + lax.shift_left(jnp.int32(1), 13 - i)
        g = count(lambda k: (k == p) & (col < cand), lambda k: (k == p) & (npast < cand))
        return jnp.where(g < need, cand, x)

    x = lax.fori_loop(0, 14, tie_body, jnp.zeros((nq, 1), I32))
    cut = jnp.where(overflow, x, INT_MAX)

    key = key_s[...]
    sel = (key > p) | ((key == p) & (col <= cut))
    nd_ref[...] = jnp.where(sel, -(npast - col).astype(F32), -jnp.inf)
    seln = (keyn > p) | ((keyn == p) & (npast <= cut))
    lane = lax.broadcasted_iota(I32, ndn_ref.shape, 1)
    ndn_ref[...] = jnp.where((lane == 0) & seln, 0.0, -jnp.inf)


def _dec_select(sc, qi_hm, ki_new, wrow):
    nq, npast = sc.shape
    return pl.pallas_call(
        _dec_select_kernel,
        out_shape=[jax.ShapeDtypeStruct((nq, npast), F32), jax.ShapeDtypeStruct((nq, PAGE), F32)],
        scratch_shapes=[pltpu.VMEM((nq, npast), I32)],
        compiler_params=pltpu.CompilerParams(vmem_limit_bytes=VMEM_LIMIT),
        name="dec_select",
    )(sc, qi_hm, ki_new, wrow)


def _dec_attn_kernel(pt_ref, q_ref, nd_ref, ndn_ref, kn_ref, vn_ref, *refs):
    k_refs, v_refs = refs[:PG], refs[PG:2 * PG]
    o_ref, m_s, l_s, acc_s = refs[2 * PG:]
    p = pl.program_id(1)
    hrow = lax.broadcasted_iota(I32, (N_HEADS, PAGE * N_KV), 0)
    ccol = lax.broadcasted_iota(I32, (N_HEADS, PAGE * N_KV), 1)
    own = (ccol % N_KV) == (hrow // GROUP)
    slope = jnp.exp2(-(hrow + 1).astype(F32))
    q = q_ref[0]

    def update(lgs, pv):
        m_old = m_s[:, 0:1]
        mx = lgs[0]
        for t in lgs[1:]:
            mx = jnp.maximum(mx, t)
        m_new = jnp.maximum(m_old, jnp.max(mx, axis=-1, keepdims=True))
        m_safe = jnp.where(m_new == -jnp.inf, 0.0, m_new)
        alpha = jnp.exp(m_old - m_safe)
        ps = [jnp.exp(t - m_safe) for t in lgs]
        ls = ps[0]
        for t in ps[1:]:
            ls = ls + t
        l_s[...] = alpha * l_s[...] + jnp.sum(ls, axis=-1, keepdims=True)
        acc_s[...] = alpha * acc_s[...] + pv(ps)
        m_s[...] = jnp.broadcast_to(m_new, m_s.shape)

    @pl.when(p == 0)
    def _():
        m_s[...] = jnp.full_like(m_s, -jnp.inf)
        l_s[...] = jnp.zeros_like(l_s)
        acc_s[...] = jnp.zeros_like(acc_s)
        ln = jnp.sum(q.astype(F32) * kn_ref[0].astype(F32), axis=-1, keepdims=True) * ATTN_SCALE
        lg = jnp.broadcast_to(ln, (N_HEADS, PAGE)) + ndn_ref[0]
        vn = vn_ref[0].astype(F32)
        update([lg], lambda ps: jnp.sum(ps[0], axis=-1, keepdims=True) * vn)

    lgs = []
    for i in range(PG):
        qk = _nt(q, k_refs[i][0].astype(BF16))
        lgs.append(jnp.where(own, qk * ATTN_SCALE + slope * nd_ref[0, i], -jnp.inf))

    def pv(ps):
        out = jnp.zeros((N_HEADS, HEAD_DIM), F32)
        for i in range(PG):
            out = out + _nn(ps[i].astype(BF16), v_refs[i][0].astype(BF16))
        return out

    update(lgs, pv)

    @pl.when(p == pl.num_programs(1) - 1)
    def _():
        l = l_s[...]
        o_ref[0] = (acc_s[...] / jnp.where(l == 0.0, 1.0, l)).astype(BF16)


def _dec_attn(page_table, q8, nd4, ndn, kn8, vn8, k_flat, v_flat):
    nbatch, npages = page_table.shape
    page_spec = lambda i: pl.BlockSpec(
        (1, PAGE * N_KV, HEAD_DIM), lambda b, p, pt: (pt[b, p * PG + i], 0, 0))
    head_spec = pl.BlockSpec((1, N_HEADS, HEAD_DIM), lambda b, p, pt: (b, 0, 0))
    return pl.pallas_call(
        _dec_attn_kernel,
        grid_spec=pltpu.PrefetchScalarGridSpec(
            num_scalar_prefetch=1,
            grid=(nbatch, npages // PG),
            in_specs=[
                head_spec,
                pl.BlockSpec((1, PG, 1, PAGE * N_KV), lambda b, p, pt: (b, p, 0, 0)),
                pl.BlockSpec((1, 1, PAGE), lambda b, p, pt: (b, 0, 0)),
                head_spec, head_spec,
            ] + [page_spec(i) for i in range(PG)] + [page_spec(i) for i in range(PG)],
            out_specs=head_spec,
            scratch_shapes=[pltpu.VMEM((N_HEADS, PAGE), F32)] * 3,
        ),
        out_shape=jax.ShapeDtypeStruct((nbatch, N_HEADS, HEAD_DIM), BF16),
        compiler_params=pltpu.CompilerParams(
            dimension_semantics=("arbitrary", "arbitrary"), vmem_limit_bytes=VMEM_LIMIT),
        name="dec_attn",
    )(page_table, q8, nd4, ndn, kn8, vn8, *([k_flat] * PG), *([v_flat] * PG))


def _dec_hgrn_kernel(s_ref, hq_ref, hf_ref, hi_ref, hg_ref, lbl_ref, ng_ref, so_ref, o_ref):
    ll0, ll1 = lbl_ref[0], lbl_ref[1]
    mx = jnp.maximum(ll0, ll1)
    e0, e1 = jnp.exp(ll0 - mx), jnp.exp(ll1 - mx)
    lb = e0 / (e0 + e1)
    f = lb + (1.0 - lb) * jax.nn.sigmoid(hf_ref[0])
    hq = hq_ref[0]
    q = hq * jax.nn.sigmoid(hq)
    ng = ng_ref[...]
    for h in range(N_HEADS):
        s_new = s_ref[0, 0, h] * f[h] + (1.0 - f[h]) * hi_ref[0, h]
        so_ref[0, h] = s_new
        o = jnp.sum(s_new * q[h], axis=0, keepdims=True)
        on = o * lax.rsqrt(jnp.mean(o * o, axis=-1, keepdims=True) + NORM_EPS) * ng
        hg = hg_ref[0, h]
        o_ref[0, h] = (on * (hg * jax.nn.sigmoid(hg))).astype(BF16)


def _dec_hgrn(state, hq_c, hf_c, hi_r, hg_r, lbl_c, norm_g):
    nbatch = state.shape[1]
    col_spec = pl.BlockSpec((1, N_HEADS, HEAD_DIM, 1), lambda b: (b, 0, 0, 0))
    row_spec = pl.BlockSpec((1, N_HEADS, 1, HEAD_DIM), lambda b: (b, 0, 0, 0))
    return pl.pallas_call(
        _dec_hgrn_kernel,
        grid=(nbatch,),
        in_specs=[
            pl.BlockSpec((1, 1, N_HEADS, HEAD_DIM, HEAD_DIM), lambda b: (0, b, 0, 0, 0)),
            col_spec, col_spec, row_spec, row_spec,
            pl.BlockSpec((2, N_HEADS, HEAD_DIM, 1), lambda b: (0, 0, 0, 0)),
            pl.BlockSpec((1, HEAD_DIM), lambda b: (0, 0)),
        ],
        out_specs=[
            pl.BlockSpec((1, N_HEADS, HEAD_DIM, HEAD_DIM), lambda b: (b, 0, 0, 0)),
            row_spec,
        ],
        out_shape=[
            jax.ShapeDtypeStruct((nbatch, N_HEADS, HEAD_DIM, HEAD_DIM), F32),
            jax.ShapeDtypeStruct((nbatch, N_HEADS, 1, HEAD_DIM), BF16),
        ],
        compiler_params=pltpu.CompilerParams(dimension_semantics=("arbitrary",)),
        name="dec_hgrn",
    )(state, hq_c, hf_c, hi_r, hg_r, lbl_c, norm_g)


def _outproj_kernel(x_ref, gi_ref, bi_ref, a_ref, h_ref, wa_ref, wh_ref, g_ref, b_ref, o_ref):
    xn = _layer_norm(x_ref[...], gi_ref[...], bi_ref[...])
    mix = _nn(a_ref[...], wa_ref[...]) + _nn(h_ref[...], wh_ref[...])
    o_ref[...] = _layer_norm(ALPHA * xn + mix, g_ref[...], b_ref[...])


def _outproj(x_all, gi, bi, attn, hgm, wa, wh, g, b, tm):
    r = x_all.shape[0]
    vec = pl.BlockSpec((1, D_MODEL), lambda i: (0, 0))
    return pl.pallas_call(
        _outproj_kernel,
        grid=(r // tm,),
        in_specs=[
            pl.BlockSpec((tm, D_MODEL), lambda i: (i, 0)), vec, vec,
            pl.BlockSpec((tm, 1024), lambda i: (i, 0)),
            pl.BlockSpec((tm, 1024), lambda i: (i, 0)),
            pl.BlockSpec((1024, D_MODEL), lambda i: (0, 0)),
            pl.BlockSpec((1024, D_MODEL), lambda i: (0, 0)),
            vec, vec,
        ],
        out_specs=pl.BlockSpec((tm, D_MODEL), lambda i: (i, 0)),
        out_shape=jax.ShapeDtypeStruct((r, D_MODEL), F32),
        compiler_params=pltpu.CompilerParams(
            dimension_semantics=("arbitrary",), vmem_limit_bytes=VMEM_LIMIT),
        name="outproj",
    )(x_all, gi, bi, attn, hgm, wa, wh, g, b)


def _ffn_kernel(x_ref, wg_ref, wu_ref, wd_ref, g_ref, b_ref, y_ref, xb_ref):
    f = pl.program_id(1)

    @pl.when(f == 0)
    def _():
        xb_ref[...] = x_ref[...].astype(BF16)
        y_ref[...] = jnp.zeros_like(y_ref)

    xb = xb_ref[...]
    gate = _nn(xb, wg_ref[...])
    up = _nn(xb, wu_ref[...])
    hid = (gate * jax.nn.sigmoid(gate) * up).astype(BF16)
    y_ref[...] += _nn(hid, wd_ref[...])

    @pl.when(f == pl.num_programs(1) - 1)
    def _():
        y_ref[...] = _layer_norm(ALPHA * x_ref[...] + y_ref[...], g_ref[...], b_ref[...])


def _ffn(x1, wg, wu, wd, g, b, tm, tf):
    r = x1.shape[0]
    dff = wg.shape[1]
    vec = pl.BlockSpec((1, D_MODEL), lambda i, f: (0, 0))
    return pl.pallas_call(
        _ffn_kernel,
        grid=(r // tm, dff // tf),
        in_specs=[
            pl.BlockSpec((tm, D_MODEL), lambda i, f: (i, 0)),
            pl.BlockSpec((D_MODEL, tf), lambda i, f: (0, f)),
            pl.BlockSpec((D_MODEL, tf), lambda i, f: (0, f)),
            pl.BlockSpec((tf, D_MODEL), lambda i, f: (f, 0)),
            vec, vec,
        ],
        out_specs=pl.BlockSpec((tm, D_MODEL), lambda i, f: (i, 0)),
        out_shape=jax.ShapeDtypeStruct((r, D_MODEL), F32),
        scratch_shapes=[pltpu.VMEM((tm, D_MODEL), BF16)],
        compiler_params=pltpu.CompilerParams(
            dimension_semantics=("arbitrary", "arbitrary"), vmem_limit_bytes=VMEM_LIMIT),
        name="ffn",
    )(x1, wg, wu, wd, g, b)


def _reorder_w_in(w):
    q, k, v, qi, wi, ki, hq, hf, hi, hg = [
        p.astype(BF16) for p in jnp.split(w.T, [1024, 1536, 2048, 2560, 2568, 2632, 3656, 4680, 5704], axis=0)]
    pad = jnp.zeros((128 - wi.shape[0], w.shape[0]), BF16)
    return jnp.concatenate([q, hq, hf, hi, hg, k, v, qi, ki, ki, wi, pad], axis=0)


def kernel(x_prompt, x_sample, cache_k, cache_v, cache_kidx, state_hgrn, page_table, meta_tokens,
           ln_in_g, ln_in_b, w_in, hg_lb_logits, hg_norm_g, w_out, ln1_g, ln1_b, w_gate, w_up,
           w_down, ln2_g, ln2_b):
    nbatch, seq, _ = x_prompt.shape
    ndec = x_sample.shape[0]
    assert w_in.shape[0] == 1 and x_sample.shape[1] == 1 and seq % BLK == 0
    tp = BLK + seq
    rp = nbatch * tp
    ntail = DEC_ROWS - ndec
    r = rp + DEC_ROWS
    assert ntail >= 0 and r % (20 * 16) == 0

    head = jnp.concatenate([jnp.zeros((N_PAD, D_MODEL), F32), meta_tokens.astype(F32)], axis=0)
    parts = []
    for b in range(nbatch):
        parts += [head, x_prompt[b]]
    x_all = jnp.concatenate(
        parts + [x_sample.reshape(ndec, D_MODEL), jnp.zeros((ntail, D_MODEL), F32)], axis=0)

    row = lambda a: a.reshape(1, -1).astype(F32)
    tm = r // 10
    z32, zb = _inproj(x_all, row(ln_in_g), row(ln_in_b), _reorder_w_in(w_in[0]), tm, 1152)

    lbl = hg_lb_logits.astype(F32)
    ng = row(hg_norm_g[0])
    attn_p = _prompt_attn(z32, zb, nbatch, tp)
    hgm_p, s_p = _prompt_hgrn(zb, lbl, ng, nbatch, tp)

    zs32, zbs = z32[rp:rp + ndec], zb[rp:rp + ndec]
    npast = page_table.shape[1] * PAGE
    wt = w_in[0].T
    w_idx = jnp.concatenate([wt[2048:2560], wt[2568:2632], wt[2560:2568],
                             jnp.zeros((56, D_MODEL), F32)], axis=0)
    z_idx = _dec_idx_proj(x_sample.reshape(ndec, D_MODEL), row(ln_in_g), row(ln_in_b), w_idx)
    qi_s = z_idx[:, 0:512].reshape(ndec, N_HEADS, IDX_DIM)
    wrow = z_idx[:, 576:576 + N_HEADS]
    kidx_t = jnp.swapaxes(cache_kidx, 2, 3)
    k_flat = cache_k[0].reshape(-1, PAGE * N_KV, HEAD_DIM)
    v_flat = cache_v[0].reshape(-1, PAGE * N_KV, HEAD_DIM)
    sc = _dec_scores(page_table, qi_s, wrow[:, :, None], kidx_t).reshape(ndec, npast)
    nd, ndn = _dec_select(sc, jnp.swapaxes(qi_s, 0, 1), z_idx[:, 512:576], wrow)
    nd4 = jnp.repeat(nd, N_KV, axis=1).reshape(ndec, npast // PAGE, 1, PAGE * N_KV)
    rep = lambda a: jnp.repeat(a.reshape(ndec, N_KV, HEAD_DIM), N_HEADS // N_KV, axis=1)
    attn_s = _dec_attn(page_table, zbs[:, C_Q:C_Q + 1024].reshape(ndec, N_HEADS, HEAD_DIM),
                       nd4, ndn.reshape(ndec, 1, PAGE),
                       rep(zbs[:, C_K:C_K + 512]), rep(zbs[:, C_V:C_V + 512]), k_flat, v_flat)
    colv = lambda c0: zbs[:, c0:c0 + 1024].astype(F32).reshape(ndec, N_HEADS, HEAD_DIM, 1)
    rowv = lambda c0: zbs[:, c0:c0 + 1024].astype(F32).reshape(ndec, N_HEADS, 1, HEAD_DIM)
    s_s, hgm_s = _dec_hgrn(state_hgrn, colv(C_HQ), colv(C_HF), rowv(C_HI), rowv(C_HG),
                           lbl.reshape(2, N_HEADS, HEAD_DIM, 1), ng)

    tail = jnp.zeros((ntail, 1024), BF16)
    attn = jnp.concatenate([attn_p, attn_s.reshape(ndec, 1024), tail], axis=0)
    hgm = jnp.concatenate([hgm_p, hgm_s.reshape(ndec, 1024), tail], axis=0)
    wo = w_out[0].astype(BF16)
    x1 = _outproj(x_all, row(ln_in_g), row(ln_in_b), attn, hgm, wo[:1024], wo[1024:],
                  row(ln1_g[0]), row(ln1_b[0]), r // 20)
    y = _ffn(x1, w_gate[0].astype(BF16), w_up[0].astype(BF16), w_down[0].astype(BF16),
             row(ln2_g[0]), row(ln2_b[0]), tm, 512)

    def prompt_cols(a, c0, width, first_row):
        return jnp.stack([lax.slice(a, (b * tp + first_row, c0), ((b + 1) * tp, c0 + width))
                          for b in range(nbatch)])

    y_prompt = prompt_cols(y, 0, D_MODEL, BLK)
    y_sample = y[rp:rp + ndec].reshape(ndec, 1, D_MODEL)
    ck, cv, ckk = C_K - C32, C_V - C32, C_KK - C32
    k_p = prompt_cols(z32, ck, 512, N_PAD).reshape(1, nbatch, tp - N_PAD, N_KV, HEAD_DIM)
    v_p = prompt_cols(z32, cv, 512, N_PAD).reshape(1, nbatch, tp - N_PAD, N_KV, HEAD_DIM)
    ki_p = prompt_cols(z32, ckk, IDX_DIM, N_PAD)[None]
    k_s = zs32[:, ck:ck + 512].reshape(1, ndec, 1, N_KV, HEAD_DIM)
    v_s = zs32[:, cv:cv + 512].reshape(1, ndec, 1, N_KV, HEAD_DIM)
    ki_s = zs32[:, ckk:ckk + IDX_DIM].reshape(1, ndec, 1, IDX_DIM)
    return (y_prompt, y_sample, k_p, v_p, ki_p, s_p[None], k_s, v_s, ki_s, s_s[None])
```

```python
import functools

import numpy as np
import jax
import jax.numpy as jnp
from jax import lax
from jax.experimental import pallas as pl
from jax.experimental.pallas import tpu as pltpu

F32 = jnp.float32
BF16 = jnp.bfloat16
I32 = jnp.int32

D_MODEL = 2048
N_META = 16
BLK = 128
N_PAD = BLK - N_META
N_HEADS = 8
N_KV = 4
HEAD_DIM = 128
IDX_DIM = 64
TOPK = 256
PAGE = 128
LN_EPS = 1e-5
NORM_EPS = 1e-6
ALPHA = 2.0 ** 0.25
ATTN_SCALE = HEAD_DIM ** -0.5
IDX_SCALE = IDX_DIM ** -0.5
IDX_W_SCALE = N_HEADS ** -0.5
INT_MIN = -2 ** 31
INT_MAX = 2 ** 31 - 1
DEC_ROWS = 192

C_Q, C_HQ, C_HF, C_HI, C_HG = 0, 1024, 2048, 3072, 4096
C_K, C_V, C_QI, C_KK, C_WI = 5120, 5632, 6144, 6656, 6784
ZW = 6912
C32 = 4608

VMEM_LIMIT = 56 * 1024 * 1024


def _nt(a, b):
    return lax.dot_general(a, b, (((1,), (1,)), ((), ())), preferred_element_type=F32)


def _nn(a, b):
    return jnp.dot(a, b, preferred_element_type=F32)


def _layer_norm(x, g, b):
    mu = jnp.mean(x, axis=-1, keepdims=True)
    xc = x - mu
    var = jnp.mean(xc * xc, axis=-1, keepdims=True)
    return xc * lax.rsqrt(var + LN_EPS) * g + b


KEY_NEG_INF = INT_MIN + 0x7FFFFF


def _key_to_float(k):
    return pltpu.bitcast(k ^ ((k >> 31) & INT_MAX), F32)


def _sort_key(s):
    s = jnp.where(s == 0.0, 0.0, s)
    bits = pltpu.bitcast(s, I32)
    return bits ^ ((bits >> 31) & INT_MAX)


def _inproj_kernel(x_ref, g_ref, b_ref, w_ref, z_ref, zb_ref, xs_ref, *, first_f32_tile):
    n = pl.program_id(1)

    @pl.when(n == 0)
    def _():
        xs_ref[...] = _layer_norm(x_ref[...], g_ref[...], b_ref[...]).astype(BF16)

    acc = _nt(xs_ref[...], w_ref[...])
    zb_ref[...] = acc.astype(BF16)

    @pl.when(n >= first_f32_tile)
    def _():
        z_ref[...] = acc


def _inproj(x_all, g, b, w, tm, tn):
    r = x_all.shape[0]
    assert C32 % tn == 0 and ZW % tn == 0
    n0 = C32 // tn
    return pl.pallas_call(
        functools.partial(_inproj_kernel, first_f32_tile=n0),
        grid=(r // tm, ZW // tn),
        in_specs=[
            pl.BlockSpec((tm, D_MODEL), lambda i, n: (i, 0)),
            pl.BlockSpec((1, D_MODEL), lambda i, n: (0, 0)),
            pl.BlockSpec((1, D_MODEL), lambda i, n: (0, 0)),
            pl.BlockSpec((tn, D_MODEL), lambda i, n: (n, 0)),
        ],
        out_specs=[
            pl.BlockSpec((tm, tn), lambda i, n: (i, jnp.maximum(n - n0, 0))),
            pl.BlockSpec((tm, tn), lambda i, n: (i, n)),
        ],
        out_shape=[jax.ShapeDtypeStruct((r, ZW - C32), F32), jax.ShapeDtypeStruct((r, ZW), BF16)],
        scratch_shapes=[pltpu.VMEM((tm, D_MODEL), BF16)],
        compiler_params=pltpu.CompilerParams(
            dimension_semantics=("arbitrary", "arbitrary"), vmem_limit_bytes=VMEM_LIMIT),
        name="inproj",
    )(x_all, g, b, w)


KCH = 3
KC = KCH * BLK
GROUP = N_HEADS // N_KV


def _fold(a):
    out = a[:, 0:BLK]
    for i in range(1, KCH):
        out = out + a[:, BLK * i:BLK * (i + 1)]
    return out


def _tile3(a):
    return jnp.concatenate([a] * KCH, axis=1)


def _prompt_attn_kernel(q_ref, qi_ref, wi_ref, k_ref, v_ref, kk_ref, o_ref,
                        sc_s, nd_s, lg_s, wb_s, qm_s, cut_s, acc_s, qs_s, m_s):
    j = pl.program_id(1)
    nch = (j + KCH) // KCH
    row = lax.broadcasted_iota(I32, (BLK, KC), 0)
    col = lax.broadcasted_iota(I32, (BLK, KC), 1)
    qrow = j * BLK + row
    zeros = jnp.zeros((BLK, BLK), F32)

    wi = wi_ref[...] * (IDX_SCALE * IDX_W_SCALE)
    qi = qi_ref[...]
    lane = lax.broadcasted_iota(I32, (BLK, BLK), 1)
    for h in range(N_HEADS):
        wb_s[h] = jnp.broadcast_to(wi[:, h:h + 1], (BLK, BLK))
        q2 = qi[:, BLK * (h // 2):BLK * (h // 2 + 1)]
        keep = (lane < IDX_DIM) if h % 2 == 0 else (lane >= IDX_DIM)
        qm_s[BLK * h:BLK * (h + 1), :] = jnp.where(keep, q2, jnp.zeros_like(q2))

    def score_body(c, carry):
        k0 = pl.multiple_of(c * KC, BLK)
        sh = _nt(qm_s[...], kk_ref[pl.ds(k0, KC), :])
        s = jnp.zeros((BLK, KC), F32)
        for h in range(N_HEADS):
            s = s + jnp.maximum(sh[BLK * h:BLK * (h + 1)], 0.0) * _tile3(wb_s[h])
        krow = k0 + col
        adm = (krow >= N_PAD) & (krow <= qrow)
        sc_s[c] = jnp.where(adm, s, -jnp.inf)
        return carry

    lax.fori_loop(0, nch, score_body, 0)

    def count(pred):
        body = lambda c, a: a + _fold(jnp.where(pred(c, sc_s[c]), 1.0, 0.0))
        return jnp.sum(lax.fori_loop(0, nch, body, zeros), axis=-1, keepdims=True)

    def radix_body(i, p):
        cand = p + lax.shift_left(jnp.int32(1), 31 - i)
        candb = jnp.broadcast_to(_key_to_float(cand), (BLK, KC))
        return jnp.where(count(lambda c, s: s >= candb) >= TOPK, cand, p)

    p0 = jnp.full((BLK, 1), INT_MIN, I32)
    p = lax.cond(j * BLK + N_META > TOPK, lambda: lax.fori_loop(0, 32, radix_body, p0), lambda: p0)
    p = jnp.maximum(p, KEY_NEG_INF)
    pf = _key_to_float(p)
    pb = jnp.broadcast_to(pf, (BLK, KC))
    def gt_ge_body(c, carry):
        s = sc_s[c]
        return (carry[0] + _fold(jnp.where(s > pb, 1.0, 0.0)), carry[1] + _fold(jnp.where(s >= pb, 1.0, 0.0)))

    a_gt, a_ge = lax.fori_loop(0, nch, gt_ge_body, (zeros, zeros))
    n_gt = jnp.sum(a_gt, axis=-1, keepdims=True)
    n_ge = jnp.sum(a_ge, axis=-1, keepdims=True)
    need = TOPK - n_gt
    overflow = (n_ge > TOPK) & (pf > -jnp.inf)

    cut_s[...] = jnp.full((BLK, KC), INT_MAX, I32)

    @pl.when(jnp.max(jnp.where(overflow, 1.0, 0.0)) > 0.0)
    def _():
        def tie_body(i, x):
            cand = x + lax.shift_left(jnp.int32(1), 12 - i)
            candb = jnp.broadcast_to(cand, (BLK, KC))
            g = count(lambda c, s: (s == pb) & (c * KC + col < candb))
            return jnp.where(g < need, cand, x)

        x = lax.fori_loop(0, 13, tie_body, jnp.zeros((BLK, 1), I32))
        cut_s[...] = jnp.broadcast_to(jnp.where(overflow, x, INT_MAX), (BLK, KC))

    cutb = cut_s[...]

    def nd_body(c, carry):
        s = sc_s[c]
        krow = c * KC + col
        sel = ((s > pb) | ((s == pb) & (krow <= cutb))) & (s > -jnp.inf)
        nd_s[c] = jnp.where(sel, -(qrow - krow).astype(F32), -jnp.inf)
        return carry

    lax.fori_loop(0, nch, nd_body, 0)

    for h in range(N_HEADS):
        qs_s[BLK * h:BLK * (h + 1), :] = q_ref[:, HEAD_DIM * h:HEAD_DIM * (h + 1)]
    m_s[...] = jnp.full_like(m_s, -jnp.inf)
    acc_s[...] = jnp.zeros_like(acc_s)
    gr = GROUP * BLK

    def logit_body(c, carry):
        k0 = pl.multiple_of(c * KC, BLK)
        nd = nd_s[c]
        for g in range(N_KV):
            qk = _nt(qs_s[gr * g:gr * (g + 1), :], k_ref[pl.ds(k0, KC), HEAD_DIM * g:HEAD_DIM * (g + 1)])
            for i in range(GROUP):
                h = GROUP * g + i
                lg = qk[BLK * i:BLK * (i + 1)] * ATTN_SCALE + 2.0 ** -(h + 1) * nd
                lg_s[c, BLK * h:BLK * (h + 1), :] = lg
                m = m_s[BLK * h:BLK * (h + 1), :]
                for t in range(KCH):
                    m = jnp.maximum(m, lg[:, BLK * t:BLK * (t + 1)])
                m_s[BLK * h:BLK * (h + 1), :] = m
        return carry

    lax.fori_loop(0, nch, logit_body, 0)
    m = jnp.max(m_s[...], axis=-1, keepdims=True)
    m_s[...] = jnp.broadcast_to(jnp.where(m == -jnp.inf, 0.0, m), m_s.shape)

    ones = jnp.ones((KC, HEAD_DIM), BF16)

    def pv_body(c, carry):
        k0 = pl.multiple_of(c * KC, BLK)
        for g in range(N_KV):
            rows = slice(gr * g, gr * (g + 1))
            pr = jnp.exp(lg_s[c, rows, :] - _tile3(m_s[rows, :])).astype(BF16)
            v1 = jnp.concatenate([v_ref[pl.ds(k0, KC), HEAD_DIM * g:HEAD_DIM * (g + 1)], ones], axis=1)
            acc_s[rows, :] += _nn(pr, v1)
        return carry

    lax.fori_loop(0, nch, pv_body, 0)
    for h in range(N_HEADS):
        acc = acc_s[BLK * h:BLK * (h + 1), :]
        l = acc[:, HEAD_DIM:2 * HEAD_DIM]
        o_ref[:, HEAD_DIM * h:HEAD_DIM * (h + 1)] = (
            acc[:, 0:HEAD_DIM] / jnp.where(l == 0.0, 1.0, l)).astype(BF16)


def _prompt_attn(z32, zb, nbatch, tp):
    nb = tp // BLK
    assert nb % KCH == 0
    nc = nb // KCH
    return pl.pallas_call(
        _prompt_attn_kernel,
        grid=(nbatch, nb),
        in_specs=[
            pl.BlockSpec((BLK, 1024), lambda b, j: (b * nb + j, C_Q // 1024)),
            pl.BlockSpec((BLK, 512), lambda b, j: (b * nb + j, C_QI // 512)),
            pl.BlockSpec((BLK, 128), lambda b, j: (b * nb + j, (C_WI - C32) // 128)),
            pl.BlockSpec((tp, 512), lambda b, j: (b, C_K // 512)),
            pl.BlockSpec((tp, 512), lambda b, j: (b, C_V // 512)),
            pl.BlockSpec((tp, 128), lambda b, j: (b, C_KK // 128)),
        ],
        out_specs=pl.BlockSpec((BLK, 1024), lambda b, j: (b * nb + j, 0)),
        out_shape=jax.ShapeDtypeStruct((nbatch * tp, 1024), BF16),
        scratch_shapes=[
            pltpu.VMEM((nc, BLK, KC), F32),
            pltpu.VMEM((nc, BLK, KC), F32),
            pltpu.VMEM((nc, N_HEADS * BLK, KC), F32),
            pltpu.VMEM((N_HEADS, BLK, BLK), F32),
            pltpu.VMEM((N_HEADS * BLK, BLK), BF16),
            pltpu.VMEM((BLK, KC), I32),
            pltpu.VMEM((N_HEADS * BLK, 2 * HEAD_DIM), F32),
            pltpu.VMEM((N_HEADS * BLK, HEAD_DIM), BF16),
            pltpu.VMEM((N_HEADS * BLK, BLK), F32),
        ],
        compiler_params=pltpu.CompilerParams(
            dimension_semantics=("arbitrary", "arbitrary"), vmem_limit_bytes=VMEM_LIMIT),
        name="prompt_attn",
    )(zb, zb, z32, zb, zb, zb)


def _scan_matrix():
    t = np.tril(np.ones((BLK, BLK), np.float32))
    mats = [t]
    for lvl in range(1, 8):
        bs = BLK >> (lvl - 1)
        r = np.arange(BLK)
        anchor = (r // bs) * bs + bs // 2 - 1
        mats.append(t[anchor])
    return np.concatenate(mats, axis=0)


def _level_signs():
    r = np.arange(BLK)
    rows = [np.where((r >> (7 - lvl)) & 1, 1.0, -1.0) for lvl in range(1, 8)]
    return np.repeat(np.concatenate(rows)[:, None], BLK, axis=1).astype(np.float32)


def _level_ids():
    t, s = np.meshgrid(np.arange(BLK), np.arange(BLK), indexing="ij")
    x = t ^ s
    hb = np.floor(np.log2(np.maximum(x, 1))).astype(np.int64)
    ids = np.where(s < t, 7 - hb, np.where(s == t, 8, 0))
    return ids.astype(np.float32)


def _prompt_hgrn_kernel(hq_ref, hf_ref, hi_ref, hg_ref, lbl_ref, ng_ref, scan_ref, sg_ref, lvl_ref,
                        o_ref, s_ref, st_s):
    c = pl.program_id(1)
    nc = pl.num_programs(1)

    @pl.when(c == 0)
    def _():
        st_s[...] = jnp.zeros_like(st_s)

    row = lax.broadcasted_iota(I32, (BLK, BLK), 0)
    valid = (c > 0) | (row >= N_PAD)
    ll0, ll1 = lbl_ref[0:1, :], lbl_ref[1:2, :]
    mx = jnp.maximum(ll0, ll1)
    e0, e1 = jnp.exp(ll0 - mx), jnp.exp(ll1 - mx)
    lb_all = e0 / (e0 + e1)
    ng = ng_ref[...]

    for h in range(N_HEADS):
        sl = slice(HEAD_DIM * h, HEAD_DIM * (h + 1))
        lb = lb_all[:, sl]
        f = lb + (1.0 - lb) * jax.nn.sigmoid(hf_ref[:, sl].astype(F32))
        lf = jnp.where(valid, jnp.log(f), 0.0)
        kf = jnp.where(valid, 1.0 - f, 0.0)
        hq = hq_ref[:, sl].astype(F32)
        qh = hq * jax.nn.sigmoid(hq)
        ih = hi_ref[:, sl]

        p0 = lf.astype(BF16)
        r1 = lf - p0.astype(F32)
        p1 = r1.astype(BF16)
        p2 = (r1 - p1.astype(F32)).astype(BF16)
        b3 = _nn(scan_ref[0:BLK, :], jnp.concatenate([p0, p1, p2], axis=1))
        b = b3[:, 0:BLK] + b3[:, BLK:2 * BLK] + b3[:, 2 * BLK:3 * BLK]
        anc = _nn(scan_ref[BLK:, :], p0)

        a = jnp.where(lvl_ref[...] == 8.0, _nt(qh.astype(BF16), kf.astype(BF16)), 0.0)
        for lvl in range(1, 8):
            rows = slice(BLK * (lvl - 1), BLK * lvl)
            dec = jnp.exp((b - anc[rows]) * sg_ref[rows, :])
            qk = _nt((qh * dec).astype(BF16), (kf * dec).astype(BF16))
            a = a + jnp.where(lvl_ref[...] == float(lvl), qk, 0.0)

        st = st_s[h]
        o = _nn(a.astype(BF16), ih) + _nt((qh * jnp.exp(b)).astype(BF16), st.astype(BF16))
        bend = b[BLK - 1:BLK, :]
        kend = (kf * jnp.exp(bend - b)).astype(BF16)
        st_new = st * jnp.exp(bend) + _nn(ih.astype(F32).T.astype(BF16), kend)
        st_s[h] = st_new

        on = o * lax.rsqrt(jnp.mean(o * o, axis=-1, keepdims=True) + NORM_EPS) * ng
        hg = hg_ref[:, sl].astype(F32)
        o_ref[:, sl] = (on * (hg * jax.nn.sigmoid(hg))).astype(BF16)

    @pl.when(c == nc - 1)
    def _():
        for h in range(N_HEADS):
            s_ref[0, h] = st_s[h].T


def _prompt_hgrn(zb, lb_logits, norm_g, nbatch, tp):
    nb = tp // BLK
    scan = jnp.asarray(_scan_matrix(), BF16)
    signs = jnp.asarray(_level_signs(), F32)
    level_ids = jnp.asarray(_level_ids(), F32)
    row_spec = lambda cb: pl.BlockSpec((BLK, 1024), lambda b, c: (b * nb + c, cb))
    return pl.pallas_call(
        _prompt_hgrn_kernel,
        grid=(nbatch, nb),
        in_specs=[
            row_spec(C_HQ // 1024), row_spec(C_HF // 1024), row_spec(C_HI // 1024), row_spec(C_HG // 1024),
            pl.BlockSpec((2, 1024), lambda b, c: (0, 0)),
            pl.BlockSpec((1, HEAD_DIM), lambda b, c: (0, 0)),
            pl.BlockSpec((8 * BLK, BLK), lambda b, c: (0, 0)),
            pl.BlockSpec((7 * BLK, BLK), lambda b, c: (0, 0)),
            pl.BlockSpec((BLK, BLK), lambda b, c: (0, 0)),
        ],
        out_specs=[
            pl.BlockSpec((BLK, 1024), lambda b, c: (b * nb + c, 0)),
            pl.BlockSpec((1, N_HEADS, HEAD_DIM, HEAD_DIM), lambda b, c: (b, 0, 0, 0)),
        ],
        out_shape=[
            jax.ShapeDtypeStruct((nbatch * tp, 1024), BF16),
            jax.ShapeDtypeStruct((nbatch, N_HEADS, HEAD_DIM, HEAD_DIM), F32),
        ],
        scratch_shapes=[pltpu.VMEM((N_HEADS, HEAD_DIM, HEAD_DIM), F32)],
        compiler_params=pltpu.CompilerParams(
            dimension_semantics=("arbitrary", "arbitrary"), vmem_limit_bytes=VMEM_LIMIT),
        name="prompt_hgrn",
    )(zb, zb, zb, zb, lb_logits, norm_g, scan, signs, level_ids)


PG = 32


def _dec_idx_proj_kernel(x_ref, g_ref, b_ref, w_ref, o_ref):
    xn = _layer_norm(x_ref[...], g_ref[...], b_ref[...])
    o_ref[...] = lax.dot_general(xn, w_ref[...], (((1,), (1,)), ((), ())),
                                 preferred_element_type=F32, precision=lax.Precision.HIGHEST)


def _dec_idx_proj(x_s, g, b, w_idx):
    return pl.pallas_call(
        _dec_idx_proj_kernel,
        out_shape=jax.ShapeDtypeStruct((x_s.shape[0], w_idx.shape[0]), F32),
        compiler_params=pltpu.CompilerParams(vmem_limit_bytes=VMEM_LIMIT),
        name="dec_idx_proj",
    )(x_s, g, b, w_idx)


PGS = 64


def _dec_score_kernel(pt_ref, qi_ref, w_ref, *refs):
    kp_refs, out_ref = refs[:PGS], refs[PGS]
    qi = qi_ref[0]
    w = w_ref[0] * (IDX_SCALE * IDX_W_SCALE)
    for i in range(PGS):
        qk = jnp.dot(qi, kp_refs[i][0, 0], preferred_element_type=F32, precision=lax.Precision.HIGHEST)
        out_ref[0, i] = jnp.sum(jnp.maximum(qk, 0.0) * w, axis=0, keepdims=True)


def _dec_scores(page_table, qi_s, wcol, kidx_t):
    nbatch, npages = page_table.shape
    page_spec = lambda i: pl.BlockSpec(
        (1, 1, IDX_DIM, PAGE), lambda b, p, pt: (0, pt[b, p * PGS + i], 0, 0))
    return pl.pallas_call(
        _dec_score_kernel,
        grid_spec=pltpu.PrefetchScalarGridSpec(
            num_scalar_prefetch=1,
            grid=(nbatch, npages // PGS),
            in_specs=[
                pl.BlockSpec((1, N_HEADS, IDX_DIM), lambda b, p, pt: (b, 0, 0)),
                pl.BlockSpec((1, N_HEADS, 1), lambda b, p, pt: (b, 0, 0)),
            ] + [page_spec(i) for i in range(PGS)],
            out_specs=pl.BlockSpec((1, PGS, 1, PAGE), lambda b, p, pt: (b, p, 0, 0)),
        ),
        out_shape=jax.ShapeDtypeStruct((nbatch, npages, 1, PAGE), F32),
        compiler_params=pltpu.CompilerParams(dimension_semantics=("arbitrary", "arbitrary")),
        name="dec_scores",
    )(page_table, qi_s, wcol, *([kidx_t] * PGS))


def _dec_select_kernel(sc_ref, qi_ref, kin_ref, w_ref, nd_ref, ndn_ref, key_s):
    nq, npast = sc_ref.shape
    col = lax.broadcasted_iota(I32, (nq, npast), 1)
    key_s[...] = _sort_key(sc_ref[...])

    kin = kin_ref[...]
    w = w_ref[...] * (IDX_SCALE * IDX_W_SCALE)
    snew = jnp.zeros((nq, 1), F32)
    for h in range(N_HEADS):
        sh = jnp.sum(qi_ref[h] * kin, axis=-1, keepdims=True)
        snew = snew + jnp.maximum(sh, 0.0) * w[:, h:h + 1]
    keyn = _sort_key(snew)

    def count(pred_past, pred_new):
        c = jnp.sum(jnp.where(pred_past(key_s[...]), 1.0, 0.0), axis=-1, keepdims=True)
        return c + jnp.where(pred_new(keyn), 1.0, 0.0)

    def radix_body(i, p):
        cand = p + lax.shift_left(jnp.int32(1), 31 - i)
        return jnp.where(count(lambda k: k >= cand, lambda k: k >= cand) >= TOPK, cand, p)

    p = lax.fori_loop(0, 32, radix_body, jnp.full((nq, 1), INT_MIN, I32))
    n_gt = count(lambda k: k > p, lambda k: k > p)
    n_ge = count(lambda k: k >= p, lambda k: k >= p)
    need = TOPK - n_gt
    overflow = n_ge > TOPK

    def tie_body(i, x):
        cand = x + lax.shift_left(jnp.int32(1), 13 - i)
        g = count(lambda k: (k == p) & (col < cand), lambda k: (k == p) & (npast < cand))
        return jnp.where(g < need, cand, x)

    x = lax.fori_loop(0, 14, tie_body, jnp.zeros((nq, 1), I32))
    cut = jnp.where(overflow, x, INT_MAX)

    key = key_s[...]
    sel = (key > p) | ((key == p) & (col <= cut))
    nd_ref[...] = jnp.where(sel, -(npast - col).astype(F32), -jnp.inf)
    seln = (keyn > p) | ((keyn == p) & (npast <= cut))
    lane = lax.broadcasted_iota(I32, ndn_ref.shape, 1)
    ndn_ref[...] = jnp.where((lane == 0) & seln, 0.0, -jnp.inf)


def _dec_select(sc, qi_hm, ki_new, wrow):
    nq, npast = sc.shape
    return pl.pallas_call(
        _dec_select_kernel,
        out_shape=[jax.ShapeDtypeStruct((nq, npast), F32), jax.ShapeDtypeStruct((nq, PAGE), F32)],
        scratch_shapes=[pltpu.VMEM((nq, npast), I32)],
        compiler_params=pltpu.CompilerParams(vmem_limit_bytes=VMEM_LIMIT),
        name="dec_select",
    )(sc, qi_hm, ki_new, wrow)


def _dec_attn_kernel(pt_ref, q_ref, nd_ref, ndn_ref, kn_ref, vn_ref, *refs):
    k_refs, v_refs = refs[:PG], refs[PG:2 * PG]
    o_ref, m_s, l_s, acc_s = refs[2 * PG:]
    p = pl.program_id(1)
    hrow = lax.broadcasted_iota(I32, (N_HEADS, PAGE * N_KV), 0)
    ccol = lax.broadcasted_iota(I32, (N_HEADS, PAGE * N_KV), 1)
    own = (ccol % N_KV) == (hrow // GROUP)
    slope = jnp.exp2(-(hrow + 1).astype(F32))
    q = q_ref[0]

    def update(lgs, pv):
        m_old = m_s[:, 0:1]
        mx = lgs[0]
        for t in lgs[1:]:
            mx = jnp.maximum(mx, t)
        m_new = jnp.maximum(m_old, jnp.max(mx, axis=-1, keepdims=True))
        m_safe = jnp.where(m_new == -jnp.inf, 0.0, m_new)
        alpha = jnp.exp(m_old - m_safe)
        ps = [jnp.exp(t - m_safe) for t in lgs]
        ls = ps[0]
        for t in ps[1:]:
            ls = ls + t
        l_s[...] = alpha * l_s[...] + jnp.sum(ls, axis=-1, keepdims=True)
        acc_s[...] = alpha * acc_s[...] + pv(ps)
        m_s[...] = jnp.broadcast_to(m_new, m_s.shape)

    @pl.when(p == 0)
    def _():
        m_s[...] = jnp.full_like(m_s, -jnp.inf)
        l_s[...] = jnp.zeros_like(l_s)
        acc_s[...] = jnp.zeros_like(acc_s)
        ln = jnp.sum(q.astype(F32) * kn_ref[0].astype(F32), axis=-1, keepdims=True) * ATTN_SCALE
        lg = jnp.broadcast_to(ln, (N_HEADS, PAGE)) + ndn_ref[0]
        vn = vn_ref[0].astype(F32)
        update([lg], lambda ps: jnp.sum(ps[0], axis=-1, keepdims=True) * vn)

    lgs = []
    for i in range(PG):
        qk = _nt(q, k_refs[i][0].astype(BF16))
        lgs.append(jnp.where(own, qk * ATTN_SCALE + slope * nd_ref[0, i], -jnp.inf))

    def pv(ps):
        out = jnp.zeros((N_HEADS, HEAD_DIM), F32)
        for i in range(PG):
            out = out + _nn(ps[i].astype(BF16), v_refs[i][0].astype(BF16))
        return out

    update(lgs, pv)

    @pl.when(p == pl.num_programs(1) - 1)
    def _():
        l = l_s[...]
        o_ref[0] = (acc_s[...] / jnp.where(l == 0.0, 1.0, l)).astype(BF16)


def _dec_attn(page_table, q8, nd4, ndn, kn8, vn8, k_flat, v_flat):
    nbatch, npages = page_table.shape
    page_spec = lambda i: pl.BlockSpec(
        (1, PAGE * N_KV, HEAD_DIM), lambda b, p, pt: (pt[b, p * PG + i], 0, 0))
    head_spec = pl.BlockSpec((1, N_HEADS, HEAD_DIM), lambda b, p, pt: (b, 0, 0))
    return pl.pallas_call(
        _dec_attn_kernel,
        grid_spec=pltpu.PrefetchScalarGridSpec(
            num_scalar_prefetch=1,
            grid=(nbatch, npages // PG),
            in_specs=[
                head_spec,
                pl.BlockSpec((1, PG, 1, PAGE * N_KV), lambda b, p, pt: (b, p, 0, 0)),
                pl.BlockSpec((1, 1, PAGE), lambda b, p, pt: (b, 0, 0)),
                head_spec, head_spec,
            ] + [page_spec(i) for i in range(PG)] + [page_spec(i) for i in range(PG)],
            out_specs=head_spec,
            scratch_shapes=[pltpu.VMEM((N_HEADS, PAGE), F32)] * 3,
        ),
        out_shape=jax.ShapeDtypeStruct((nbatch, N_HEADS, HEAD_DIM), BF16),
        compiler_params=pltpu.CompilerParams(
            dimension_semantics=("arbitrary", "arbitrary"), vmem_limit_bytes=VMEM_LIMIT),
        name="dec_attn",
    )(page_table, q8, nd4, ndn, kn8, vn8, *([k_flat] * PG), *([v_flat] * PG))


def _dec_hgrn_kernel(s_ref, hq_ref, hf_ref, hi_ref, hg_ref, lbl_ref, ng_ref, so_ref, o_ref):
    ll0, ll1 = lbl_ref[0], lbl_ref[1]
    mx = jnp.maximum(ll0, ll1)
    e0, e1 = jnp.exp(ll0 - mx), jnp.exp(ll1 - mx)
    lb = e0 / (e0 + e1)
    f = lb + (1.0 - lb) * jax.nn.sigmoid(hf_ref[0])
    hq = hq_ref[0]
    q = hq * jax.nn.sigmoid(hq)
    ng = ng_ref[...]
    for h in range(N_HEADS):
        s_new = s_ref[0, 0, h] * f[h] + (1.0 - f[h]) * hi_ref[0, h]
        so_ref[0, h] = s_new
        o = jnp.sum(s_new * q[h], axis=0, keepdims=True)
        on = o * lax.rsqrt(jnp.mean(o * o, axis=-1, keepdims=True) + NORM_EPS) * ng
        hg = hg_ref[0, h]
        o_ref[0, h] = (on * (hg * jax.nn.sigmoid(hg))).astype(BF16)


def _dec_hgrn(state, hq_c, hf_c, hi_r, hg_r, lbl_c, norm_g):
    nbatch = state.shape[1]
    col_spec = pl.BlockSpec((1, N_HEADS, HEAD_DIM, 1), lambda b: (b, 0, 0, 0))
    row_spec = pl.BlockSpec((1, N_HEADS, 1, HEAD_DIM), lambda b: (b, 0, 0, 0))
    return pl.pallas_call(
        _dec_hgrn_kernel,
        grid=(nbatch,),
        in_specs=[
            pl.BlockSpec((1, 1, N_HEADS, HEAD_DIM, HEAD_DIM), lambda b: (0, b, 0, 0, 0)),
            col_spec, col_spec, row_spec, row_spec,
            pl.BlockSpec((2, N_HEADS, HEAD_DIM, 1), lambda b: (0, 0, 0, 0)),
            pl.BlockSpec((1, HEAD_DIM), lambda b: (0, 0)),
        ],
        out_specs=[
            pl.BlockSpec((1, N_HEADS, HEAD_DIM, HEAD_DIM), lambda b: (b, 0, 0, 0)),
            row_spec,
        ],
        out_shape=[
            jax.ShapeDtypeStruct((nbatch, N_HEADS, HEAD_DIM, HEAD_DIM), F32),
            jax.ShapeDtypeStruct((nbatch, N_HEADS, 1, HEAD_DIM), BF16),
        ],
        compiler_params=pltpu.CompilerParams(dimension_semantics=("arbitrary",)),
        name="dec_hgrn",
    )(state, hq_c, hf_c, hi_r, hg_r, lbl_c, norm_g)


def _outproj_kernel(x_ref, gi_ref, bi_ref, a_ref, h_ref, wa_ref, wh_ref, g_ref, b_ref, o_ref):
    xn = _layer_norm(x_ref[...], gi_ref[...], bi_ref[...])
    mix = _nn(a_ref[...], wa_ref[...]) + _nn(h_ref[...], wh_ref[...])
    o_ref[...] = _layer_norm(ALPHA * xn + mix, g_ref[...], b_ref[...])


def _outproj(x_all, gi, bi, attn, hgm, wa, wh, g, b, tm):
    r = x_all.shape[0]
    vec = pl.BlockSpec((1, D_MODEL), lambda i: (0, 0))
    return pl.pallas_call(
        _outproj_kernel,
        grid=(r // tm,),
        in_specs=[
            pl.BlockSpec((tm, D_MODEL), lambda i: (i, 0)), vec, vec,
            pl.BlockSpec((tm, 1024), lambda i: (i, 0)),
            pl.BlockSpec((tm, 1024), lambda i: (i, 0)),
            pl.BlockSpec((1024, D_MODEL), lambda i: (0, 0)),
            pl.BlockSpec((1024, D_MODEL), lambda i: (0, 0)),
            vec, vec,
        ],
        out_specs=pl.BlockSpec((tm, D_MODEL), lambda i: (i, 0)),
        out_shape=jax.ShapeDtypeStruct((r, D_MODEL), F32),
        compiler_params=pltpu.CompilerParams(
            dimension_semantics=("arbitrary",), vmem_limit_bytes=VMEM_LIMIT),
        name="outproj",
    )(x_all, gi, bi, attn, hgm, wa, wh, g, b)


def _ffn_kernel(x_ref, wg_ref, wu_ref, wd_ref, g_ref, b_ref, y_ref, xb_ref):
    f = pl.program_id(1)

    @pl.when(f == 0)
    def _():
        xb_ref[...] = x_ref[...].astype(BF16)
        y_ref[...] = jnp.zeros_like(y_ref)

    xb = xb_ref[...]
    gate = _nn(xb, wg_ref[...])
    up = _nn(xb, wu_ref[...])
    hid = (gate * jax.nn.sigmoid(gate) * up).astype(BF16)
    y_ref[...] += _nn(hid, wd_ref[...])

    @pl.when(f == pl.num_programs(1) - 1)
    def _():
        y_ref[...] = _layer_norm(ALPHA * x_ref[...] + y_ref[...], g_ref[...], b_ref[...])


def _ffn(x1, wg, wu, wd, g, b, tm, tf):
    r = x1.shape[0]
    dff = wg.shape[1]
    vec = pl.BlockSpec((1, D_MODEL), lambda i, f: (0, 0))
    return pl.pallas_call(
        _ffn_kernel,
        grid=(r // tm, dff // tf),
        in_specs=[
            pl.BlockSpec((tm, D_MODEL), lambda i, f: (i, 0)),
            pl.BlockSpec((D_MODEL, tf), lambda i, f: (0, f)),
            pl.BlockSpec((D_MODEL, tf), lambda i, f: (0, f)),
            pl.BlockSpec((tf, D_MODEL), lambda i, f: (f, 0)),
            vec, vec,
        ],
        out_specs=pl.BlockSpec((tm, D_MODEL), lambda i, f: (i, 0)),
        out_shape=jax.ShapeDtypeStruct((r, D_MODEL), F32),
        scratch_shapes=[pltpu.VMEM((tm, D_MODEL), BF16)],
        compiler_params=pltpu.CompilerParams(
            dimension_semantics=("arbitrary", "arbitrary"), vmem_limit_bytes=VMEM_LIMIT),
        name="ffn",
    )(x1, wg, wu, wd, g, b)


def _reorder_w_in(w):
    q, k, v, qi, wi, ki, hq, hf, hi, hg = [
        p.astype(BF16) for p in jnp.split(w.T, [1024, 1536, 2048, 2560, 2568, 2632, 3656, 4680, 5704], axis=0)]
    pad = jnp.zeros((128 - wi.shape[0], w.shape[0]), BF16)
    return jnp.concatenate([q, hq, hf, hi, hg, k, v, qi, ki, ki, wi, pad], axis=0)


def kernel(x_prompt, x_sample, cache_k, cache_v, cache_kidx, state_hgrn, page_table, meta_tokens,
           ln_in_g, ln_in_b, w_in, hg_lb_logits, hg_norm_g, w_out, ln1_g, ln1_b, w_gate, w_up,
           w_down, ln2_g, ln2_b):
    nbatch, seq, _ = x_prompt.shape
    ndec = x_sample.shape[0]
    assert w_in.shape[0] == 1 and x_sample.shape[1] == 1 and seq % BLK == 0
    tp = BLK + seq
    rp = nbatch * tp
    ntail = DEC_ROWS - ndec
    r = rp + DEC_ROWS
    assert ntail >= 0 and r % (20 * 16) == 0

    head = jnp.concatenate([jnp.zeros((N_PAD, D_MODEL), F32), meta_tokens.astype(F32)], axis=0)
    parts = []
    for b in range(nbatch):
        parts += [head, x_prompt[b]]
    x_all = jnp.concatenate(
        parts + [x_sample.reshape(ndec, D_MODEL), jnp.zeros((ntail, D_MODEL), F32)], axis=0)

    row = lambda a: a.reshape(1, -1).astype(F32)
    tm = r // 10
    z32, zb = _inproj(x_all, row(ln_in_g), row(ln_in_b), _reorder_w_in(w_in[0]), tm, 1152)

    lbl = hg_lb_logits.astype(F32)
    ng = row(hg_norm_g[0])
    attn_p = _prompt_attn(z32, zb, nbatch, tp)
    hgm_p, s_p = _prompt_hgrn(zb, lbl, ng, nbatch, tp)

    zs32, zbs = z32[rp:rp + ndec], zb[rp:rp + ndec]
    npast = page_table.shape[1] * PAGE
    wt = w_in[0].T
    w_idx = jnp.concatenate([wt[2048:2560], wt[2568:2632], wt[2560:2568],
                             jnp.zeros((56, D_MODEL), F32)], axis=0)
    z_idx = _dec_idx_proj(x_sample.reshape(ndec, D_MODEL), row(ln_in_g), row(ln_in_b), w_idx)
    qi_s = z_idx[:, 0:512].reshape(ndec, N_HEADS, IDX_DIM)
    wrow = z_idx[:, 576:576 + N_HEADS]
    kidx_t = jnp.swapaxes(cache_kidx, 2, 3)
    k_flat = cache_k[0].reshape(-1, PAGE * N_KV, HEAD_DIM)
    v_flat = cache_v[0].reshape(-1, PAGE * N_KV, HEAD_DIM)
    sc = _dec_scores(page_table, qi_s, wrow[:, :, None], kidx_t).reshape(ndec, npast)
    nd, ndn = _dec_select(sc, jnp.swapaxes(qi_s, 0, 1), z_idx[:, 512:576], wrow)
    nd4 = jnp.repeat(nd, N_KV, axis=1).reshape(ndec, npast // PAGE, 1, PAGE * N_KV)
    rep = lambda a: jnp.repeat(a.reshape(ndec, N_KV, HEAD_DIM), N_HEADS // N_KV, axis=1)
    attn_s = _dec_attn(page_table, zbs[:, C_Q:C_Q + 1024].reshape(ndec, N_HEADS, HEAD_DIM),
                       nd4, ndn.reshape(ndec, 1, PAGE),
                       rep(zbs[:, C_K:C_K + 512]), rep(zbs[:, C_V:C_V + 512]), k_flat, v_flat)
    colv = lambda c0: zbs[:, c0:c0 + 1024].astype(F32).reshape(ndec, N_HEADS, HEAD_DIM, 1)
    rowv = lambda c0: zbs[:, c0:c0 + 1024].astype(F32).reshape(ndec, N_HEADS, 1, HEAD_DIM)
    s_s, hgm_s = _dec_hgrn(state_hgrn, colv(C_HQ), colv(C_HF), rowv(C_HI), rowv(C_HG),
                           lbl.reshape(2, N_HEADS, HEAD_DIM, 1), ng)

    tail = jnp.zeros((ntail, 1024), BF16)
    attn = jnp.concatenate([attn_p, attn_s.reshape(ndec, 1024), tail], axis=0)
    hgm = jnp.concatenate([hgm_p, hgm_s.reshape(ndec, 1024), tail], axis=0)
    wo = w_out[0].astype(BF16)
    x1 = _outproj(x_all, row(ln_in_g), row(ln_in_b), attn, hgm, wo[:1024], wo[1024:],
                  row(ln1_g[0]), row(ln1_b[0]), r // 20)
    y = _ffn(x1, w_gate[0].astype(BF16), w_up[0].astype(BF16), w_down[0].astype(BF16),
             row(ln2_g[0]), row(ln2_b[0]), tm, 512)

    def prompt_cols(a, c0, width, first_row):
        return jnp.stack([lax.slice(a, (b * tp + first_row, c0), ((b + 1) * tp, c0 + width))
                          for b in range(nbatch)])

    y_prompt = prompt_cols(y, 0, D_MODEL, BLK)
    y_sample = y[rp:rp + ndec].reshape(ndec, 1, D_MODEL)
    ck, cv, ckk = C_K - C32, C_V - C32, C_KK - C32
    k_p = prompt_cols(z32, ck, 512, N_PAD).reshape(1, nbatch, tp - N_PAD, N_KV, HEAD_DIM)
    v_p = prompt_cols(z32, cv, 512, N_PAD).reshape(1, nbatch, tp - N_PAD, N_KV, HEAD_DIM)
    ki_p = prompt_cols(z32, ckk, IDX_DIM, N_PAD)[None]
    k_s = zs32[:, ck:ck + 512].reshape(1, ndec, 1, N_KV, HEAD_DIM)
    v_s = zs32[:, cv:cv + 512].reshape(1, ndec, 1, N_KV, HEAD_DIM)
    ki_s = zs32[:, ckk:ckk + IDX_DIM].reshape(1, ndec, 1, IDX_DIM)
    return (y_prompt, y_sample, k_p, v_p, ki_p, s_p[None], k_s, v_s, ki_s, s_s[None])
```

```python
import functools

import numpy as np
import jax
import jax.numpy as jnp
from jax import lax
from jax.experimental import pallas as pl
from jax.experimental.pallas import tpu as pltpu

F32 = jnp.float32
BF16 = jnp.bfloat16
I32 = jnp.int32

D_MODEL = 2048
N_META = 16
BLK = 128
N_PAD = BLK - N_META
N_HEADS = 8
N_KV = 4
HEAD_DIM = 128
IDX_DIM = 64
TOPK = 256
PAGE = 128
LN_EPS = 1e-5
NORM_EPS = 1e-6
ALPHA = 2.0 ** 0.25
ATTN_SCALE = HEAD_DIM ** -0.5
IDX_SCALE = IDX_DIM ** -0.5
IDX_W_SCALE = N_HEADS ** -0.5
INT_MIN = -2 ** 31
INT_MAX = 2 ** 31 - 1
DEC_ROWS = 192

C_Q, C_HQ, C_HF, C_HI, C_HG = 0, 1024, 2048, 3072, 4096
C_K, C_V, C_QI, C_KK, C_WI = 5120, 5632, 6144, 6656, 6784
ZW = 6912
C32 = 4608

VMEM_LIMIT = 56 * 1024 * 1024


def _nt(a, b):
    return lax.dot_general(a, b, (((1,), (1,)), ((), ())), preferred_element_type=F32)


def _nn(a, b):
    return jnp.dot(a, b, preferred_element_type=F32)


def _layer_norm(x, g, b):
    mu = jnp.mean(x, axis=-1, keepdims=True)
    xc = x - mu
    var = jnp.mean(xc * xc, axis=-1, keepdims=True)
    return xc * lax.rsqrt(var + LN_EPS) * g + b


KEY_NEG_INF = INT_MIN + 0x7FFFFF


def _key_to_float(k):
    return pltpu.bitcast(k ^ ((k >> 31) & INT_MAX), F32)


def _sort_key(s):
    s = jnp.where(s == 0.0, 0.0, s)
    bits = pltpu.bitcast(s, I32)
    return bits ^ ((bits >> 31) & INT_MAX)


def _inproj_kernel(x_ref, g_ref, b_ref, w_ref, z_ref, zb_ref, xs_ref, *, first_f32_tile):
    n = pl.program_id(1)

    @pl.when(n == 0)
    def _():
        xs_ref[...] = _layer_norm(x_ref[...], g_ref[...], b_ref[...]).astype(BF16)

    acc = _nt(xs_ref[...], w_ref[...])
    zb_ref[...] = acc.astype(BF16)

    @pl.when(n >= first_f32_tile)
    def _():
        z_ref[...] = acc


def _inproj(x_all, g, b, w, tm, tn):
    r = x_all.shape[0]
    assert C32 % tn == 0 and ZW % tn == 0
    n0 = C32 // tn
    return pl.pallas_call(
        functools.partial(_inproj_kernel, first_f32_tile=n0),
        grid=(r // tm, ZW // tn),
        in_specs=[
            pl.BlockSpec((tm, D_MODEL), lambda i, n: (i, 0)),
            pl.BlockSpec((1, D_MODEL), lambda i, n: (0, 0)),
            pl.BlockSpec((1, D_MODEL), lambda i, n: (0, 0)),
            pl.BlockSpec((tn, D_MODEL), lambda i, n: (n, 0)),
        ],
        out_specs=[
            pl.BlockSpec((tm, tn), lambda i, n: (i, jnp.maximum(n - n0, 0))),
            pl.BlockSpec((tm, tn), lambda i, n: (i, n)),
        ],
        out_shape=[jax.ShapeDtypeStruct((r, ZW - C32), F32), jax.ShapeDtypeStruct((r, ZW), BF16)],
        scratch_shapes=[pltpu.VMEM((tm, D_MODEL), BF16)],
        compiler_params=pltpu.CompilerParams(
            dimension_semantics=("arbitrary", "arbitrary"), vmem_limit_bytes=VMEM_LIMIT),
        name="inproj",
    )(x_all, g, b, w)


KCH = 3
KC = KCH * BLK
GROUP = N_HEADS // N_KV


def _fold(a):
    out = a[:, 0:BLK]
    for i in range(1, KCH):
        out = out + a[:, BLK * i:BLK * (i + 1)]
    return out


def _tile3(a):
    return jnp.concatenate([a] * KCH, axis=1)


def _prompt_attn_kernel(q_ref, qi_ref, wi_ref, k_ref, v_ref, kk_ref, o_ref,
                        sc_s, lg_s, wb_s, qm_s, cut_s, acc_s, qs_s, m_s):
    j = pl.program_id(1)
    nch = (j + KCH) // KCH
    row = lax.broadcasted_iota(I32, (BLK, KC), 0)
    col = lax.broadcasted_iota(I32, (BLK, KC), 1)
    qrow = j * BLK + row
    zeros = jnp.zeros((BLK, BLK), F32)

    wi = wi_ref[...] * (IDX_SCALE * IDX_W_SCALE)
    qi = qi_ref[...]
    lane = lax.broadcasted_iota(I32, (BLK, BLK), 1)
    for h in range(N_HEADS):
        wb_s[h] = jnp.broadcast_to(wi[:, h:h + 1], (BLK, BLK))
        q2 = qi[:, BLK * (h // 2):BLK * (h // 2 + 1)]
        keep = (lane < IDX_DIM) if h % 2 == 0 else (lane >= IDX_DIM)
        qm_s[BLK * h:BLK * (h + 1), :] = jnp.where(keep, q2, jnp.zeros_like(q2))

    def score_body(c, carry):
        k0 = pl.multiple_of(c * KC, BLK)
        sh = _nt(qm_s[...], kk_ref[pl.ds(k0, KC), :])
        s = jnp.zeros((BLK, KC), F32)
        for h in range(N_HEADS):
            s = s + jnp.maximum(sh[BLK * h:BLK * (h + 1)], 0.0) * _tile3(wb_s[h])
        krow = k0 + col
        adm = (krow >= N_PAD) & (krow <= qrow)
        sc_s[c] = jnp.where(adm, s, -jnp.inf)
        return carry

    lax.fori_loop(0, nch, score_body, 0)

    def count(pred):
        body = lambda c, a: a + _fold(jnp.where(pred(c, sc_s[c]), 1.0, 0.0))
        return jnp.sum(lax.fori_loop(0, nch, body, zeros), axis=-1, keepdims=True)

    def radix_body(i, p):
        cand = p + lax.shift_left(jnp.int32(1), 31 - i)
        candb = jnp.broadcast_to(_key_to_float(cand), (BLK, KC))
        return jnp.where(count(lambda c, s: s >= candb) >= TOPK, cand, p)

    p0 = jnp.full((BLK, 1), INT_MIN, I32)
    p = lax.cond(j * BLK + N_META > TOPK, lambda: lax.fori_loop(0, 32, radix_body, p0), lambda: p0)
    p = jnp.maximum(p, KEY_NEG_INF)
    pf = _key_to_float(p)
    pb = jnp.broadcast_to(pf, (BLK, KC))
    def gt_ge_body(c, carry):
        s = sc_s[c]
        return (carry[0] + _fold(jnp.where(s > pb, 1.0, 0.0)), carry[1] + _fold(jnp.where(s >= pb, 1.0, 0.0)))

    a_gt, a_ge = lax.fori_loop(0, nch, gt_ge_body, (zeros, zeros))
    n_gt = jnp.sum(a_gt, axis=-1, keepdims=True)
    n_ge = jnp.sum(a_ge, axis=-1, keepdims=True)
    need = TOPK - n_gt
    overflow = (n_ge > TOPK) & (pf > -jnp.inf)

    cut_s[...] = jnp.full((BLK, KC), INT_MAX, I32)

    @pl.when(jnp.max(jnp.where(overflow, 1.0, 0.0)) > 0.0)
    def _():
        def tie_body(i, x):
            cand = x + lax.shift_left(jnp.int32(1), 12 - i)
            candb = jnp.broadcast_to(cand, (BLK, KC))
            g = count(lambda c, s: (s == pb) & (c * KC + col < candb))
            return jnp.where(g < need, cand, x)

        x = lax.fori_loop(0, 13, tie_body, jnp.zeros((BLK, 1), I32))
        cut_s[...] = jnp.broadcast_to(jnp.where(overflow, x, INT_MAX), (BLK, KC))

    cutb = cut_s[...]

    def neg_dist(c):
        s = sc_s[c]
        krow = c * KC + col
        sel = ((s > pb) | ((s == pb) & (krow <= cutb))) & (s > -jnp.inf)
        return jnp.where(sel, -(qrow - krow).astype(F32), -jnp.inf)

    for h in range(N_HEADS):
        qs_s[BLK * h:BLK * (h + 1), :] = q_ref[:, HEAD_DIM * h:HEAD_DIM * (h + 1)]
    m_s[...] = jnp.full_like(m_s, -jnp.inf)
    acc_s[...] = jnp.zeros_like(acc_s)
    gr = GROUP * BLK

    def logit_body(c, carry):
        k0 = pl.multiple_of(c * KC, BLK)
        nd = neg_dist(c)
        for g in range(N_KV):
            qk = _nt(qs_s[gr * g:gr * (g + 1), :], k_ref[pl.ds(k0, KC), HEAD_DIM * g:HEAD_DIM * (g + 1)])
            for i in range(GROUP):
                h = GROUP * g + i
                lg = qk[BLK * i:BLK * (i + 1)] * ATTN_SCALE + 2.0 ** -(h + 1) * nd
                lg_s[c, BLK * h:BLK * (h + 1), :] = lg
                m = m_s[BLK * h:BLK * (h + 1), :]
                for t in range(KCH):
                    m = jnp.maximum(m, lg[:, BLK * t:BLK * (t + 1)])
                m_s[BLK * h:BLK * (h + 1), :] = m
        return carry

    lax.fori_loop(0, nch, logit_body, 0)
    m = jnp.max(m_s[...], axis=-1, keepdims=True)
    m_s[...] = jnp.broadcast_to(jnp.where(m == -jnp.inf, 0.0, m), m_s.shape)

    ones = jnp.ones((KC, HEAD_DIM), BF16)

    def pv_body(c, carry):
        k0 = pl.multiple_of(c * KC, BLK)
        for g in range(N_KV):
            rows = slice(gr * g, gr * (g + 1))
            pr = jnp.exp(lg_s[c, rows, :] - _tile3(m_s[rows, :])).astype(BF16)
            v1 = jnp.concatenate([v_ref[pl.ds(k0, KC), HEAD_DIM * g:HEAD_DIM * (g + 1)], ones], axis=1)
            acc_s[rows, :] += _nn(pr, v1)
        return carry

    lax.fori_loop(0, nch, pv_body, 0)
    for h in range(N_HEADS):
        acc = acc_s[BLK * h:BLK * (h + 1), :]
        l = acc[:, HEAD_DIM:2 * HEAD_DIM]
        o_ref[:, HEAD_DIM * h:HEAD_DIM * (h + 1)] = (
            acc[:, 0:HEAD_DIM] / jnp.where(l == 0.0, 1.0, l)).astype(BF16)


def _prompt_attn(z32, zb, nbatch, tp):
    nb = tp // BLK
    assert nb % KCH == 0
    nc = nb // KCH
    return pl.pallas_call(
        _prompt_attn_kernel,
        grid=(nbatch, nb),
        in_specs=[
            pl.BlockSpec((BLK, 1024), lambda b, j: (b * nb + j, C_Q // 1024)),
            pl.BlockSpec((BLK, 512), lambda b, j: (b * nb + j, C_QI // 512)),
            pl.BlockSpec((BLK, 128), lambda b, j: (b * nb + j, (C_WI - C32) // 128)),
            pl.BlockSpec((tp, 512), lambda b, j: (b, C_K // 512)),
            pl.BlockSpec((tp, 512), lambda b, j: (b, C_V // 512)),
            pl.BlockSpec((tp, 128), lambda b, j: (b, C_KK // 128)),
        ],
        out_specs=pl.BlockSpec((BLK, 1024), lambda b, j: (b * nb + j, 0)),
        out_shape=jax.ShapeDtypeStruct((nbatch * tp, 1024), BF16),
        scratch_shapes=[
            pltpu.VMEM((nc, BLK, KC), F32),
            pltpu.VMEM((nc, N_HEADS * BLK, KC), F32),
            pltpu.VMEM((N_HEADS, BLK, BLK), F32),
            pltpu.VMEM((N_HEADS * BLK, BLK), BF16),
            pltpu.VMEM((BLK, KC), I32),
            pltpu.VMEM((N_HEADS * BLK, 2 * HEAD_DIM), F32),
            pltpu.VMEM((N_HEADS * BLK, HEAD_DIM), BF16),
            pltpu.VMEM((N_HEADS * BLK, BLK), F32),
        ],
        compiler_params=pltpu.CompilerParams(
            dimension_semantics=("arbitrary", "arbitrary"), vmem_limit_bytes=VMEM_LIMIT),
        name="prompt_attn",
    )(zb, zb, z32, zb, zb, zb)


def _scan_matrix():
    t = np.tril(np.ones((BLK, BLK), np.float32))
    mats = [t]
    for lvl in range(1, 8):
        bs = BLK >> (lvl - 1)
        r = np.arange(BLK)
        anchor = (r // bs) * bs + bs // 2 - 1
        mats.append(t[anchor])
    return np.concatenate(mats, axis=0)


def _level_signs():
    r = np.arange(BLK)
    rows = [np.where((r >> (7 - lvl)) & 1, 1.0, -1.0) for lvl in range(1, 8)]
    return np.repeat(np.concatenate(rows)[:, None], BLK, axis=1).astype(np.float32)


def _level_ids():
    t, s = np.meshgrid(np.arange(BLK), np.arange(BLK), indexing="ij")
    x = t ^ s
    hb = np.floor(np.log2(np.maximum(x, 1))).astype(np.int64)
    ids = np.where(s < t, 7 - hb, np.where(s == t, 8, 0))
    return ids.astype(np.float32)


def _prompt_hgrn_kernel(hq_ref, hf_ref, hi_ref, hg_ref, lbl_ref, ng_ref, scan_ref, sg_ref, lvl_ref,
                        o_ref, s_ref, st_s):
    c = pl.program_id(1)
    nc = pl.num_programs(1)

    @pl.when(c == 0)
    def _():
        st_s[...] = jnp.zeros_like(st_s)

    row = lax.broadcasted_iota(I32, (BLK, BLK), 0)
    valid = (c > 0) | (row >= N_PAD)
    ll0, ll1 = lbl_ref[0:1, :], lbl_ref[1:2, :]
    mx = jnp.maximum(ll0, ll1)
    e0, e1 = jnp.exp(ll0 - mx), jnp.exp(ll1 - mx)
    lb_all = e0 / (e0 + e1)
    ng = ng_ref[...]

    for h in range(N_HEADS):
        sl = slice(HEAD_DIM * h, HEAD_DIM * (h + 1))
        lb = lb_all[:, sl]
        f = lb + (1.0 - lb) * jax.nn.sigmoid(hf_ref[:, sl].astype(F32))
        lf = jnp.where(valid, jnp.log(f), 0.0)
        kf = jnp.where(valid, 1.0 - f, 0.0)
        hq = hq_ref[:, sl].astype(F32)
        qh = hq * jax.nn.sigmoid(hq)
        ih = hi_ref[:, sl]

        p0 = lf.astype(BF16)
        r1 = lf - p0.astype(F32)
        p1 = r1.astype(BF16)
        p2 = (r1 - p1.astype(F32)).astype(BF16)
        b3 = _nn(scan_ref[0:BLK, :], jnp.concatenate([p0, p1, p2], axis=1))
        b = b3[:, 0:BLK] + b3[:, BLK:2 * BLK] + b3[:, 2 * BLK:3 * BLK]
        anc = _nn(scan_ref[BLK:, :], p0)

        a = jnp.where(lvl_ref[...] == 8.0, _nt(qh.astype(BF16), kf.astype(BF16)), 0.0)
        for lvl in range(1, 8):
            rows = slice(BLK * (lvl - 1), BLK * lvl)
            dec = jnp.exp((b - anc[rows]) * sg_ref[rows, :])
            qk = _nt((qh * dec).astype(BF16), (kf * dec).astype(BF16))
            a = a + jnp.where(lvl_ref[...] == float(lvl), qk, 0.0)

        st = st_s[h]
        o = _nn(a.astype(BF16), ih) + _nt((qh * jnp.exp(b)).astype(BF16), st.astype(BF16))
        bend = b[BLK - 1:BLK, :]
        kend = (kf * jnp.exp(bend - b)).astype(BF16)
        st_new = st * jnp.exp(bend) + _nn(ih.astype(F32).T.astype(BF16), kend)
        st_s[h] = st_new

        on = o * lax.rsqrt(jnp.mean(o * o, axis=-1, keepdims=True) + NORM_EPS) * ng
        hg = hg_ref[:, sl].astype(F32)
        o_ref[:, sl] = (on * (hg * jax.nn.sigmoid(hg))).astype(BF16)

    @pl.when(c == nc - 1)
    def _():
        for h in range(N_HEADS):
            s_ref[0, h] = st_s[h].T


def _prompt_hgrn(zb, lb_logits, norm_g, nbatch, tp):
    nb = tp // BLK
    scan = jnp.asarray(_scan_matrix(), BF16)
    signs = jnp.asarray(_level_signs(), F32)
    level_ids = jnp.asarray(_level_ids(), F32)
    row_spec = lambda cb: pl.BlockSpec((BLK, 1024), lambda b, c: (b * nb + c, cb))
    return pl.pallas_call(
        _prompt_hgrn_kernel,
        grid=(nbatch, nb),
        in_specs=[
            row_spec(C_HQ // 1024), row_spec(C_HF // 1024), row_spec(C_HI // 1024), row_spec(C_HG // 1024),
            pl.BlockSpec((2, 1024), lambda b, c: (0, 0)),
            pl.BlockSpec((1, HEAD_DIM), lambda b, c: (0, 0)),
            pl.BlockSpec((8 * BLK, BLK), lambda b, c: (0, 0)),
            pl.BlockSpec((7 * BLK, BLK), lambda b, c: (0, 0)),
            pl.BlockSpec((BLK, BLK), lambda b, c: (0, 0)),
        ],
        out_specs=[
            pl.BlockSpec((BLK, 1024), lambda b, c: (b * nb + c, 0)),
            pl.BlockSpec((1, N_HEADS, HEAD_DIM, HEAD_DIM), lambda b, c: (b, 0, 0, 0)),
        ],
        out_shape=[
            jax.ShapeDtypeStruct((nbatch * tp, 1024), BF16),
            jax.ShapeDtypeStruct((nbatch, N_HEADS, HEAD_DIM, HEAD_DIM), F32),
        ],
        scratch_shapes=[pltpu.VMEM((N_HEADS, HEAD_DIM, HEAD_DIM), F32)],
        compiler_params=pltpu.CompilerParams(
            dimension_semantics=("arbitrary", "arbitrary"), vmem_limit_bytes=VMEM_LIMIT),
        name="prompt_hgrn",
    )(zb, zb, zb, zb, lb_logits, norm_g, scan, signs, level_ids)


PG = 32


def _dec_idx_proj_kernel(x_ref, g_ref, b_ref, w_ref, o_ref):
    xn = _layer_norm(x_ref[...], g_ref[...], b_ref[...])
    o_ref[...] = lax.dot_general(xn, w_ref[...], (((1,), (1,)), ((), ())),
                                 preferred_element_type=F32, precision=lax.Precision.HIGHEST)


def _dec_idx_proj(x_s, g, b, w_idx):
    return pl.pallas_call(
        _dec_idx_proj_kernel,
        out_shape=jax.ShapeDtypeStruct((x_s.shape[0], w_idx.shape[0]), F32),
        compiler_params=pltpu.CompilerParams(vmem_limit_bytes=VMEM_LIMIT),
        name="dec_idx_proj",
    )(x_s, g, b, w_idx)


PGS = 64


def _dec_score_kernel(pt_ref, qi_ref, w_ref, *refs):
    kp_refs, out_ref = refs[:PGS], refs[PGS]
    qi = qi_ref[0]
    w = w_ref[0] * (IDX_SCALE * IDX_W_SCALE)
    for i in range(PGS):
        qk = jnp.dot(qi, kp_refs[i][0, 0], preferred_element_type=F32, precision=lax.Precision.HIGHEST)
        out_ref[0, i] = jnp.sum(jnp.maximum(qk, 0.0) * w, axis=0, keepdims=True)


def _dec_scores(page_table, qi_s, wcol, kidx_t):
    nbatch, npages = page_table.shape
    page_spec = lambda i: pl.BlockSpec(
        (1, 1, IDX_DIM, PAGE), lambda b, p, pt: (0, pt[b, p * PGS + i], 0, 0))
    return pl.pallas_call(
        _dec_score_kernel,
        grid_spec=pltpu.PrefetchScalarGridSpec(
            num_scalar_prefetch=1,
            grid=(nbatch, npages // PGS),
            in_specs=[
                pl.BlockSpec((1, N_HEADS, IDX_DIM), lambda b, p, pt: (b, 0, 0)),
                pl.BlockSpec((1, N_HEADS, 1), lambda b, p, pt: (b, 0, 0)),
            ] + [page_spec(i) for i in range(PGS)],
            out_specs=pl.BlockSpec((1, PGS, 1, PAGE), lambda b, p, pt: (b, p, 0, 0)),
        ),
        out_shape=jax.ShapeDtypeStruct((nbatch, npages, 1, PAGE), F32),
        compiler_params=pltpu.CompilerParams(dimension_semantics=("arbitrary", "arbitrary")),
        name="dec_scores",
    )(page_table, qi_s, wcol, *([kidx_t] * PGS))


def _dec_select_kernel(sc_ref, qi_ref, kin_ref, w_ref, nd_ref, ndn_ref, key_s):
    nq, npast = sc_ref.shape
    col = lax.broadcasted_iota(I32, (nq, npast), 1)
    key_s[...] = _sort_key(sc_ref[...])

    kin = kin_ref[...]
    w = w_ref[...] * (IDX_SCALE * IDX_W_SCALE)
    snew = jnp.zeros((nq, 1), F32)
    for h in range(N_HEADS):
        sh = jnp.sum(qi_ref[h] * kin, axis=-1, keepdims=True)
        snew = snew + jnp.maximum(sh, 0.0) * w[:, h:h + 1]
    keyn = _sort_key(snew)

    def count(pred_past, pred_new):
        c = jnp.sum(jnp.where(pred_past(key_s[...]), 1.0, 0.0), axis=-1, keepdims=True)
        return c + jnp.where(pred_new(keyn), 1.0, 0.0)

    def radix_body(i, p):
        cand = p + lax.shift_left(jnp.int32(1), 31 - i)
        return jnp.where(count(lambda k: k >= cand, lambda k: k >= cand) >= TOPK, cand, p)

    p = lax.fori_loop(0, 32, radix_body, jnp.full((nq, 1), INT_MIN, I32))
    n_gt = count(lambda k: k > p, lambda k: k > p)
    n_ge = count(lambda k: k >= p, lambda k: k >= p)
    need = TOPK - n_gt
    overflow = n_ge > TOPK

    def tie_body(i, x):
        cand = x + lax.shift_left(jnp.int32(1), 13 - i)
        g = count(lambda k: (k == p) & (col < cand), lambda k: (k == p) & (npast < cand))
        return jnp.where(g < need, cand, x)

    x = lax.fori_loop(0, 14, tie_body, jnp.zeros((nq, 1), I32))
    cut = jnp.where(overflow, x, INT_MAX)

    key = key_s[...]
    sel = (key > p) | ((key == p) & (col <= cut))
    nd_ref[...] = jnp.where(sel, -(npast - col).astype(F32), -jnp.inf)
    seln = (keyn > p) | ((keyn == p) & (npast <= cut))
    lane = lax.broadcasted_iota(I32, ndn_ref.shape, 1)
    ndn_ref[...] = jnp.where((lane == 0) & seln, 0.0, -jnp.inf)


def _dec_select(sc, qi_hm, ki_new, wrow):
    nq, npast = sc.shape
    return pl.pallas_call(
        _dec_select_kernel,
        out_shape=[jax.ShapeDtypeStruct((nq, npast), F32), jax.ShapeDtypeStruct((nq, PAGE), F32)],
        scratch_shapes=[pltpu.VMEM((nq, npast), I32)],
        compiler_params=pltpu.CompilerParams(vmem_limit_bytes=VMEM_LIMIT),
        name="dec_select",
    )(sc, qi_hm, ki_new, wrow)


def _dec_attn_kernel(pt_ref, q_ref, nd_ref, ndn_ref, kn_ref, vn_ref, *refs):
    k_refs, v_refs = refs[:PG], refs[PG:2 * PG]
    o_ref, m_s, l_s, acc_s = refs[2 * PG:]
    p = pl.program_id(1)
    hrow = lax.broadcasted_iota(I32, (N_HEADS, PAGE * N_KV), 0)
    ccol = lax.broadcasted_iota(I32, (N_HEADS, PAGE * N_KV), 1)
    own = (ccol % N_KV) == (hrow // GROUP)
    slope = jnp.exp2(-(hrow + 1).astype(F32))
    q = q_ref[0]

    def update(lgs, pv):
        m_old = m_s[:, 0:1]
        mx = lgs[0]
        for t in lgs[1:]:
            mx = jnp.maximum(mx, t)
        m_new = jnp.maximum(m_old, jnp.max(mx, axis=-1, keepdims=True))
        m_safe = jnp.where(m_new == -jnp.inf, 0.0, m_new)
        alpha = jnp.exp(m_old - m_safe)
        ps = [jnp.exp(t - m_safe) for t in lgs]
        ls = ps[0]
        for t in ps[1:]:
            ls = ls + t
        l_s[...] = alpha * l_s[...] + jnp.sum(ls, axis=-1, keepdims=True)
        acc_s[...] = alpha * acc_s[...] + pv(ps)
        m_s[...] = jnp.broadcast_to(m_new, m_s.shape)

    @pl.when(p == 0)
    def _():
        m_s[...] = jnp.full_like(m_s, -jnp.inf)
        l_s[...] = jnp.zeros_like(l_s)
        acc_s[...] = jnp.zeros_like(acc_s)
        ln = jnp.sum(q.astype(F32) * kn_ref[0].astype(F32), axis=-1, keepdims=True) * ATTN_SCALE
        lg = jnp.broadcast_to(ln, (N_HEADS, PAGE)) + ndn_ref[0]
        vn = vn_ref[0].astype(F32)
        update([lg], lambda ps: jnp.sum(ps[0], axis=-1, keepdims=True) * vn)

    lgs = []
    for i in range(PG):
        qk = _nt(q, k_refs[i][0].astype(BF16))
        lgs.append(jnp.where(own, qk * ATTN_SCALE + slope * nd_ref[0, i], -jnp.inf))

    def pv(ps):
        out = jnp.zeros((N_HEADS, HEAD_DIM), F32)
        for i in range(PG):
            out = out + _nn(ps[i].astype(BF16), v_refs[i][0].astype(BF16))
        return out

    update(lgs, pv)

    @pl.when(p == pl.num_programs(1) - 1)
    def _():
        l = l_s[...]
        o_ref[0] = (acc_s[...] / jnp.where(l == 0.0, 1.0, l)).astype(BF16)


def _dec_attn(page_table, q8, nd4, ndn, kn8, vn8, k_flat, v_flat):
    nbatch, npages = page_table.shape
    page_spec = lambda i: pl.BlockSpec(
        (1, PAGE * N_KV, HEAD_DIM), lambda b, p, pt: (pt[b, p * PG + i], 0, 0))
    head_spec = pl.BlockSpec((1, N_HEADS, HEAD_DIM), lambda b, p, pt: (b, 0, 0))
    return pl.pallas_call(
        _dec_attn_kernel,
        grid_spec=pltpu.PrefetchScalarGridSpec(
            num_scalar_prefetch=1,
            grid=(nbatch, npages // PG),
            in_specs=[
                head_spec,
                pl.BlockSpec((1, PG, 1, PAGE * N_KV), lambda b, p, pt: (b, p, 0, 0)),
                pl.BlockSpec((1, 1, PAGE), lambda b, p, pt: (b, 0, 0)),
                head_spec, head_spec,
            ] + [page_spec(i) for i in range(PG)] + [page_spec(i) for i in range(PG)],
            out_specs=head_spec,
            scratch_shapes=[pltpu.VMEM((N_HEADS, PAGE), F32)] * 3,
        ),
        out_shape=jax.ShapeDtypeStruct((nbatch, N_HEADS, HEAD_DIM), BF16),
        compiler_params=pltpu.CompilerParams(
            dimension_semantics=("arbitrary", "arbitrary"), vmem_limit_bytes=VMEM_LIMIT),
        name="dec_attn",
    )(page_table, q8, nd4, ndn, kn8, vn8, *([k_flat] * PG), *([v_flat] * PG))


def _dec_hgrn_kernel(s_ref, hq_ref, hf_ref, hi_ref, hg_ref, lbl_ref, ng_ref, so_ref, o_ref):
    ll0, ll1 = lbl_ref[0], lbl_ref[1]
    mx = jnp.maximum(ll0, ll1)
    e0, e1 = jnp.exp(ll0 - mx), jnp.exp(ll1 - mx)
    lb = e0 / (e0 + e1)
    f = lb + (1.0 - lb) * jax.nn.sigmoid(hf_ref[0])
    hq = hq_ref[0]
    q = hq * jax.nn.sigmoid(hq)
    ng = ng_ref[...]
    for h in range(N_HEADS):
        s_new = s_ref[0, 0, h] * f[h] + (1.0 - f[h]) * hi_ref[0, h]
        so_ref[0, h] = s_new
        o = jnp.sum(s_new * q[h], axis=0, keepdims=True)
        on = o * lax.rsqrt(jnp.mean(o * o, axis=-1, keepdims=True) + NORM_EPS) * ng
        hg = hg_ref[0, h]
        o_ref[0, h] = (on * (hg * jax.nn.sigmoid(hg))).astype(BF16)


def _dec_hgrn(state, hq_c, hf_c, hi_r, hg_r, lbl_c, norm_g):
    nbatch = state.shape[1]
    col_spec = pl.BlockSpec((1, N_HEADS, HEAD_DIM, 1), lambda b: (b, 0, 0, 0))
    row_spec = pl.BlockSpec((1, N_HEADS, 1, HEAD_DIM), lambda b: (b, 0, 0, 0))
    return pl.pallas_call(
        _dec_hgrn_kernel,
        grid=(nbatch,),
        in_specs=[
            pl.BlockSpec((1, 1, N_HEADS, HEAD_DIM, HEAD_DIM), lambda b: (0, b, 0, 0, 0)),
            col_spec, col_spec, row_spec, row_spec,
            pl.BlockSpec((2, N_HEADS, HEAD_DIM, 1), lambda b: (0, 0, 0, 0)),
            pl.BlockSpec((1, HEAD_DIM), lambda b: (0, 0)),
        ],
        out_specs=[
            pl.BlockSpec((1, N_HEADS, HEAD_DIM, HEAD_DIM), lambda b: (b, 0, 0, 0)),
            row_spec,
        ],
        out_shape=[
            jax.ShapeDtypeStruct((nbatch, N_HEADS, HEAD_DIM, HEAD_DIM), F32),
            jax.ShapeDtypeStruct((nbatch, N_HEADS, 1, HEAD_DIM), BF16),
        ],
        compiler_params=pltpu.CompilerParams(dimension_semantics=("arbitrary",)),
        name="dec_hgrn",
    )(state, hq_c, hf_c, hi_r, hg_r, lbl_c, norm_g)


def _outproj_kernel(x_ref, gi_ref, bi_ref, a_ref, h_ref, wa_ref, wh_ref, g_ref, b_ref, o_ref):
    xn = _layer_norm(x_ref[...], gi_ref[...], bi_ref[...])
    mix = _nn(a_ref[...], wa_ref[...]) + _nn(h_ref[...], wh_ref[...])
    o_ref[...] = _layer_norm(ALPHA * xn + mix, g_ref[...], b_ref[...])


def _outproj(x_all, gi, bi, attn, hgm, wa, wh, g, b, tm):
    r = x_all.shape[0]
    vec = pl.BlockSpec((1, D_MODEL), lambda i: (0, 0))
    return pl.pallas_call(
        _outproj_kernel,
        grid=(r // tm,),
        in_specs=[
            pl.BlockSpec((tm, D_MODEL), lambda i: (i, 0)), vec, vec,
            pl.BlockSpec((tm, 1024), lambda i: (i, 0)),
            pl.BlockSpec((tm, 1024), lambda i: (i, 0)),
            pl.BlockSpec((1024, D_MODEL), lambda i: (0, 0)),
            pl.BlockSpec((1024, D_MODEL), lambda i: (0, 0)),
            vec, vec,
        ],
        out_specs=pl.BlockSpec((tm, D_MODEL), lambda i: (i, 0)),
        out_shape=jax.ShapeDtypeStruct((r, D_MODEL), F32),
        compiler_params=pltpu.CompilerParams(
            dimension_semantics=("arbitrary",), vmem_limit_bytes=VMEM_LIMIT),
        name="outproj",
    )(x_all, gi, bi, attn, hgm, wa, wh, g, b)


def _ffn_kernel(x_ref, wg_ref, wu_ref, wd_ref, g_ref, b_ref, y_ref, xb_ref):
    f = pl.program_id(1)

    @pl.when(f == 0)
    def _():
        xb_ref[...] = x_ref[...].astype(BF16)
        y_ref[...] = jnp.zeros_like(y_ref)

    xb = xb_ref[...]
    gate = _nn(xb, wg_ref[...])
    up = _nn(xb, wu_ref[...])
    hid = (gate * jax.nn.sigmoid(gate) * up).astype(BF16)
    y_ref[...] += _nn(hid, wd_ref[...])

    @pl.when(f == pl.num_programs(1) - 1)
    def _():
        y_ref[...] = _layer_norm(ALPHA * x_ref[...] + y_ref[...], g_ref[...], b_ref[...])


def _ffn(x1, wg, wu, wd, g, b, tm, tf):
    r = x1.shape[0]
    dff = wg.shape[1]
    vec = pl.BlockSpec((1, D_MODEL), lambda i, f: (0, 0))
    return pl.pallas_call(
        _ffn_kernel,
        grid=(r // tm, dff // tf),
        in_specs=[
            pl.BlockSpec((tm, D_MODEL), lambda i, f: (i, 0)),
            pl.BlockSpec((D_MODEL, tf), lambda i, f: (0, f)),
            pl.BlockSpec((D_MODEL, tf), lambda i, f: (0, f)),
            pl.BlockSpec((tf, D_MODEL), lambda i, f: (f, 0)),
            vec, vec,
        ],
        out_specs=pl.BlockSpec((tm, D_MODEL), lambda i, f: (i, 0)),
        out_shape=jax.ShapeDtypeStruct((r, D_MODEL), F32),
        scratch_shapes=[pltpu.VMEM((tm, D_MODEL), BF16)],
        compiler_params=pltpu.CompilerParams(
            dimension_semantics=("arbitrary", "arbitrary"), vmem_limit_bytes=VMEM_LIMIT),
        name="ffn",
    )(x1, wg, wu, wd, g, b)


def _reorder_w_in(w):
    q, k, v, qi, wi, ki, hq, hf, hi, hg = [
        p.astype(BF16) for p in jnp.split(w.T, [1024, 1536, 2048, 2560, 2568, 2632, 3656, 4680, 5704], axis=0)]
    pad = jnp.zeros((128 - wi.shape[0], w.shape[0]), BF16)
    return jnp.concatenate([q, hq, hf, hi, hg, k, v, qi, ki, ki, wi, pad], axis=0)


def kernel(x_prompt, x_sample, cache_k, cache_v, cache_kidx, state_hgrn, page_table, meta_tokens,
           ln_in_g, ln_in_b, w_in, hg_lb_logits, hg_norm_g, w_out, ln1_g, ln1_b, w_gate, w_up,
           w_down, ln2_g, ln2_b):
    nbatch, seq, _ = x_prompt.shape
    ndec = x_sample.shape[0]
    assert w_in.shape[0] == 1 and x_sample.shape[1] == 1 and seq % BLK == 0
    tp = BLK + seq
    rp = nbatch * tp
    ntail = DEC_ROWS - ndec
    r = rp + DEC_ROWS
    assert ntail >= 0 and r % (20 * 16) == 0

    head = jnp.concatenate([jnp.zeros((N_PAD, D_MODEL), F32), meta_tokens.astype(F32)], axis=0)
    parts = []
    for b in range(nbatch):
        parts += [head, x_prompt[b]]
    x_all = jnp.concatenate(
        parts + [x_sample.reshape(ndec, D_MODEL), jnp.zeros((ntail, D_MODEL), F32)], axis=0)

    row = lambda a: a.reshape(1, -1).astype(F32)
    tm = r // 10
    z32, zb = _inproj(x_all, row(ln_in_g), row(ln_in_b), _reorder_w_in(w_in[0]), tm, 1152)

    lbl = hg_lb_logits.astype(F32)
    ng = row(hg_norm_g[0])
    attn_p = _prompt_attn(z32, zb, nbatch, tp)
    hgm_p, s_p = _prompt_hgrn(zb, lbl, ng, nbatch, tp)

    zs32, zbs = z32[rp:rp + ndec], zb[rp:rp + ndec]
    npast = page_table.shape[1] * PAGE
    wt = w_in[0].T
    w_idx = jnp.concatenate([wt[2048:2560], wt[2568:2632], wt[2560:2568],
                             jnp.zeros((56, D_MODEL), F32)], axis=0)
    z_idx = _dec_idx_proj(x_sample.reshape(ndec, D_MODEL), row(ln_in_g), row(ln_in_b), w_idx)
    qi_s = z_idx[:, 0:512].reshape(ndec, N_HEADS, IDX_DIM)
    wrow = z_idx[:, 576:576 + N_HEADS]
    kidx_t = jnp.swapaxes(cache_kidx, 2, 3)
    k_flat = cache_k[0].reshape(-1, PAGE * N_KV, HEAD_DIM)
    v_flat = cache_v[0].reshape(-1, PAGE * N_KV, HEAD_DIM)
    sc = _dec_scores(page_table, qi_s, wrow[:, :, None], kidx_t).reshape(ndec, npast)
    nd, ndn = _dec_select(sc, jnp.swapaxes(qi_s, 0, 1), z_idx[:, 512:576], wrow)
    nd4 = jnp.repeat(nd, N_KV, axis=1).reshape(ndec, npast // PAGE, 1, PAGE * N_KV)
    rep = lambda a: jnp.repeat(a.reshape(ndec, N_KV, HEAD_DIM), N_HEADS // N_KV, axis=1)
    attn_s = _dec_attn(page_table, zbs[:, C_Q:C_Q + 1024].reshape(ndec, N_HEADS, HEAD_DIM),
                       nd4, ndn.reshape(ndec, 1, PAGE),
                       rep(zbs[:, C_K:C_K + 512]), rep(zbs[:, C_V:C_V + 512]), k_flat, v_flat)
    colv = lambda c0: zbs[:, c0:c0 + 1024].astype(F32).reshape(ndec, N_HEADS, HEAD_DIM, 1)
    rowv = lambda c0: zbs[:, c0:c0 + 1024].astype(F32).reshape(ndec, N_HEADS, 1, HEAD_DIM)
    s_s, hgm_s = _dec_hgrn(state_hgrn, colv(C_HQ), colv(C_HF), rowv(C_HI), rowv(C_HG),
                           lbl.reshape(2, N_HEADS, HEAD_DIM, 1), ng)

    tail = jnp.zeros((ntail, 1024), BF16)
    attn = jnp.concatenate([attn_p, attn_s.reshape(ndec, 1024), tail], axis=0)
    hgm = jnp.concatenate([hgm_p, hgm_s.reshape(ndec, 1024), tail], axis=0)
    wo = w_out[0].astype(BF16)
    x1 = _outproj(x_all, row(ln_in_g), row(ln_in_b), attn, hgm, wo[:1024], wo[1024:],
                  row(ln1_g[0]), row(ln1_b[0]), r // 20)
    y = _ffn(x1, w_gate[0].astype(BF16), w_up[0].astype(BF16), w_down[0].astype(BF16),
             row(ln2_g[0]), row(ln2_b[0]), tm, 512)

    def prompt_cols(a, c0, width, first_row):
        return jnp.stack([lax.slice(a, (b * tp + first_row, c0), ((b + 1) * tp, c0 + width))
                          for b in range(nbatch)])

    y_prompt = prompt_cols(y, 0, D_MODEL, BLK)
    y_sample = y[rp:rp + ndec].reshape(ndec, 1, D_MODEL)
    ck, cv, ckk = C_K - C32, C_V - C32, C_KK - C32
    k_p = prompt_cols(z32, ck, 512, N_PAD).reshape(1, nbatch, tp - N_PAD, N_KV, HEAD_DIM)
    v_p = prompt_cols(z32, cv, 512, N_PAD).reshape(1, nbatch, tp - N_PAD, N_KV, HEAD_DIM)
    ki_p = prompt_cols(z32, ckk, IDX_DIM, N_PAD)[None]
    k_s = zs32[:, ck:ck + 512].reshape(1, ndec, 1, N_KV, HEAD_DIM)
    v_s = zs32[:, cv:cv + 512].reshape(1, ndec, 1, N_KV, HEAD_DIM)
    ki_s = zs32[:, ckk:ckk + IDX_DIM].reshape(1, ndec, 1, IDX_DIM)
    return (y_prompt, y_sample, k_p, v_p, ki_p, s_p[None], k_s, v_s, ki_s, s_s[None])
```
